```python
import math
import jax, jax.numpy as jnp
from jax import lax
import numpy as np

D_MODEL = 1024
BATCH = 4
SEQ = 8192
DEPTH = 2

CONF_WIDTH = 512
CONF_KERNEL = 31
SSM_D_INNER = 1024
SSM_HEADDIM = 64
SSM_HEADS = SSM_D_INNER // SSM_HEADDIM
SSM_GROUPS = 2
SSM_STATE = 128
SSM_CONV = 4
SSM_CHUNK = 128
SSM_XBC = SSM_D_INNER + 2 * SSM_GROUPS * SSM_STATE
ATTN_HEADS = 8
ATTN_HEAD_DIM = 64
ATTN_WIDTH = ATTN_HEADS * ATTN_HEAD_DIM
MOBA_BLOCK = 256
MOBA_TOPK = 3
MOBA_Q_CHUNK = 64
FFN_DENSE = 2816
N_EXPERTS = 8
MOE_TOPK = 2
FFN_EXPERT = 3584
N_BRANCHES = 3
EPS = 1e-6
IN_SPLITS = (2 * CONF_WIDTH, SSM_D_INNER, SSM_XBC, SSM_HEADS, 3 * ATTN_WIDTH, N_BRANCHES * D_MODEL)
IN_WIDTH = 2 * CONF_WIDTH + SSM_D_INNER + SSM_XBC + SSM_HEADS + 3 * ATTN_WIDTH + N_BRANCHES * D_MODEL

kernel_name = 'hybrid_conformer_ssd_moba_moe_block'


def _split(a, sizes):
    idx = np.cumsum(np.array(sizes))[:-1].tolist()
    return jnp.split(a, idx, axis=-1)


def _rmsnorm(x, w):
    xf = x.astype(jnp.float32)
    y = xf * lax.rsqrt(jnp.mean(xf * xf, axis=-1, keepdims=True) + EPS)
    return (y * w.astype(jnp.float32)).astype(x.dtype)


def _layernorm(x, w, b):
    xf = x.astype(jnp.float32)
    mu = jnp.mean(xf, axis=-1, keepdims=True)
    xc = xf - mu
    y = xc * lax.rsqrt(jnp.mean(xc * xc, axis=-1, keepdims=True) + EPS)
    return (y * w.astype(jnp.float32) + b.astype(jnp.float32)).astype(x.dtype)


def _causal_dwconv(x, w, b):
    k = w.shape[0]
    y = lax.conv_general_dilated(
        x, w[:, None, :].astype(x.dtype), window_strides=(1,), padding=((k - 1, 0),),
        dimension_numbers=('NWC', 'WIO', 'NWC'), feature_group_count=x.shape[-1])
    return y + b


def _swiglu(h, wg, wu, wd):
    return (jax.nn.silu(h @ wg) * (h @ wu)) @ wd


def _ssd(x, dt, a, bm, cm, d_skip):
    bsz, s, h, p = x.shape
    g, n = bm.shape[2], bm.shape[3]
    r = h // g
    lc = SSM_CHUNK
    nc = s // lc
    xr = x.reshape(bsz, nc, lc, g, r, p)
    dtr = dt.reshape(bsz, nc, lc, g, r)
    br = bm.reshape(bsz, nc, lc, g, n)
    cr = cm.reshape(bsz, nc, lc, g, n)
    ad = (dtr * a.reshape(g, r)).astype(jnp.float32).transpose(0, 3, 4, 1, 2)
    a_cum = jnp.cumsum(ad, axis=-1)
    xdt = xr * dtr[..., None]
    causal = jnp.tril(jnp.ones((lc, lc), dtype=bool))
    seg = a_cum[..., :, None] - a_cum[..., None, :]
    lmat = jnp.where(causal, jnp.exp(jnp.where(causal, seg, 0.0)), 0.0)
    cb = jnp.einsum('bclgn,bcsgn->bgcls', cr, br)
    y_diag = jnp.einsum('bgrcls,bcsgrp->bclgrp', cb[:, :, None] * lmat, xdt)
    decay_states = jnp.exp(a_cum[..., -1:] - a_cum)
    states = jnp.einsum('bcsgn,bgrcs,bcsgrp->bcgrpn', br, decay_states, xdt)
    chunk_decay = jnp.exp(a_cum[..., -1])

    def step(hstate, inp):
        s_c, d_c = inp
        return hstate * d_c[..., None, None] + s_c, hstate

    h0 = jnp.zeros((bsz, g, r, p, n), dtype=states.dtype)
    _, prev = lax.scan(step, h0, (jnp.moveaxis(states, 1, 0), jnp.moveaxis(chunk_decay, -1, 0)))
    y_off = jnp.einsum('bclgn,cbgrpn,bgrcl->bclgrp', cr, prev, jnp.exp(a_cum))
    y = y_diag + y_off + xr * d_skip.reshape(g, r)[:, :, None]
    return y.reshape(bsz, s, h * p)


def _gated_group_rmsnorm(y, z, w):
    gz = (y * jax.nn.silu(z)).astype(jnp.float32)
    shp = gz.shape
    gg = gz.reshape(shp[:-1] + (SSM_GROUPS, shp[-1] // SSM_GROUPS))
    gg = gg * lax.rsqrt(jnp.mean(gg * gg, axis=-1, keepdims=True) + EPS)
    return (gg.reshape(shp) * w.astype(jnp.float32)).astype(y.dtype)


def _moba(q, k, v):
    bsz, s, h, dh = q.shape
    nb = -(-s // MOBA_BLOCK)
    pad = nb * MOBA_BLOCK - s
    k_eff = min(MOBA_TOPK, nb)
    scale = ATTN_HEAD_DIM ** -0.5
    qt = q.transpose(0, 2, 1, 3)
    kt = jnp.pad(k.transpose(0, 2, 1, 3), ((0, 0), (0, 0), (0, pad), (0, 0)))
    vt = jnp.pad(v.transpose(0, 2, 1, 3), ((0, 0), (0, 0), (0, pad), (0, 0)))
    kb = kt.reshape(bsz, h, nb, MOBA_BLOCK, dh)
    vb = vt.reshape(bsz, h, nb, MOBA_BLOCK, dh)
    kmean = jnp.mean(kb.astype(jnp.float32), axis=3)
    bi = jnp.arange(bsz)[:, None, None, None]
    hi = jnp.arange(h)[None, :, None, None]

    def chunk(i):
        q0 = i * MOBA_Q_CHUNK
        blk = q0 // MOBA_BLOCK
        qc = lax.dynamic_slice_in_dim(qt, q0, MOBA_Q_CHUNK, axis=2)
        qpos = q0 + jnp.arange(MOBA_Q_CHUNK)
        gate = jnp.einsum('bhqd,bhnd->bhqn', qc.astype(jnp.float32), kmean)
        gate = jnp.where(jnp.arange(nb) < blk, gate, -jnp.inf)
        _, top_i = lax.top_k(gate, k_eff)
        valid = jnp.arange(k_eff) < blk
        kg = kb[bi, hi, top_i]
        vg = vb[bi, hi, top_i]
        s_sel = jnp.einsum('bhqd,bhqjkd->bhqjk', qc, kg).astype(jnp.float32) * scale
        s_sel = jnp.where(valid[:, None], s_sel, -jnp.inf)
        s_sel = s_sel.reshape(bsz, h, MOBA_Q_CHUNK, k_eff * MOBA_BLOCK)
        k_own = lax.dynamic_slice_in_dim(kt, blk * MOBA_BLOCK, MOBA_BLOCK, axis=2)
        v_own = lax.dynamic_slice_in_dim(vt, blk * MOBA_BLOCK, MOBA_BLOCK, axis=2)
        kpos = blk * MOBA_BLOCK + jnp.arange(MOBA_BLOCK)
        s_own = jnp.einsum('bhqd,bhkd->bhqk', qc, k_own).astype(jnp.float32) * scale
        s_own = jnp.where(kpos[None, :] <= qpos[:, None], s_own, -jnp.inf)
        prob = jax.nn.softmax(jnp.concatenate([s_sel, s_own], axis=-1), axis=-1).astype(v.dtype)
        p_sel = prob[..., :k_eff * MOBA_BLOCK].reshape(bsz, h, MOBA_Q_CHUNK, k_eff, MOBA_BLOCK)
        p_own = prob[..., k_eff * MOBA_BLOCK:]
        return (jnp.einsum('bhqjk,bhqjkd->bhqd', p_sel, vg)
                + jnp.einsum('bhqk,bhkd->bhqd', p_own, v_own))

    out = lax.map(chunk, jnp.arange(s // MOBA_Q_CHUNK))
    return out.transpose(1, 0, 3, 2, 4).reshape(bsz, s, h * dh)


def _token_mixer(h, w_in, conf_conv_w, conf_conv_b, conf_ln_w, conf_ln_b, conf_out,
                 ssm_conv_w, ssm_conv_b, ssm_dt_bias, ssm_A_log, ssm_D, ssm_norm_w, ssm_out,
                 attn_out, mix_out):
    bsz, s, _ = h.shape
    proj = h @ w_in
    conf_in, z, xbc, dt_raw, qkv, gate_in = _split(proj, IN_SPLITS)
    val, glu_gate = jnp.split(conf_in, 2, axis=-1)
    u = _causal_dwconv(val * jax.nn.sigmoid(glu_gate), conf_conv_w, conf_conv_b)
    u = jax.nn.silu(_layernorm(u, conf_ln_w, conf_ln_b))
    out_a = u @ conf_out
    xbc = jax.nn.silu(_causal_dwconv(xbc, ssm_conv_w, ssm_conv_b))
    xs, bm, cm = _split(xbc, (SSM_D_INNER, SSM_GROUPS * SSM_STATE, SSM_GROUPS * SSM_STATE))
    dt = jax.nn.softplus((dt_raw + ssm_dt_bias).astype(jnp.float32))
    a = -jnp.exp(ssm_A_log.astype(jnp.float32))
    y = _ssd(xs.reshape(bsz, s, SSM_HEADS, SSM_HEADDIM), dt, a,
             bm.reshape(bsz, s, SSM_GROUPS, SSM_STATE), cm.reshape(bsz, s, SSM_GROUPS, SSM_STATE), ssm_D)
    out_b = _gated_group_rmsnorm(y.astype(h.dtype), z, ssm_norm_w) @ ssm_out
    q, k, v = [t.reshape(bsz, s, ATTN_HEADS, ATTN_HEAD_DIM) for t in jnp.split(qkv, 3, axis=-1)]
    out_c = _moba(q, k, v).astype(h.dtype) @ attn_out
    g_a, g_b, g_c = jnp.split(jax.nn.sigmoid(gate_in), 3, axis=-1)
    return (g_a * out_a + g_b * out_b + g_c * out_c) @ mix_out


def _moe(h, router_w, router_b, wg, wu, wd):
    logits = (h @ router_w + router_b).astype(jnp.float32)
    top_v, top_i = lax.top_k(logits, MOE_TOPK)
    top_w = jax.nn.softmax(top_v, axis=-1)
    gates = jnp.sum(top_w[..., None] * jax.nn.one_hot(top_i, N_EXPERTS, dtype=jnp.float32), axis=-2)
    gates = gates.astype(h.dtype)
    out = jnp.zeros_like(h)
    for e in range(N_EXPERTS):
        out = out + gates[..., e:e + 1] * _swiglu(h, wg[e], wu[e], wd[e])
    return out


def setup_inputs(seed: int = 0) -> dict:
    key = jax.random.key(seed)
    ks = jax.random.split(key, 40)
    f32 = jnp.float32
    L = DEPTH
    n_dense = (DEPTH + 1) // 2
    n_moe = DEPTH // 2

    def nrm(k, shape, scale):
        return jax.random.normal(k, shape, f32) * scale

    dt0 = jnp.exp(jax.random.uniform(ks[14], (L, SSM_HEADS), f32, math.log(1e-3), math.log(1e-1)))
    return {
        'x': nrm(ks[0], (BATCH, SEQ, D_MODEL), 1.0),
        'c': nrm(ks[1], (BATCH, D_MODEL), 1.0),
        'norm1_w': 1.0 + nrm(ks[2], (L, D_MODEL), 0.02),
        'norm2_w': 1.0 + nrm(ks[3], (L, D_MODEL), 0.02),
        'ada_w': nrm(ks[4], (L, D_MODEL, 6 * D_MODEL), D_MODEL ** -0.5),
        'ada_b': nrm(ks[5], (L, 6 * D_MODEL), 0.01),
        'w_in': nrm(ks[6], (L, D_MODEL, IN_WIDTH), D_MODEL ** -0.5),
        'conf_conv_w': nrm(ks[7], (L, CONF_KERNEL, CONF_WIDTH), CONF_KERNEL ** -0.5),
        'conf_conv_b': nrm(ks[8], (L, CONF_WIDTH), 0.01),
        'conf_ln_w': 1.0 + nrm(ks[9], (L, CONF_WIDTH), 0.02),
        'conf_ln_b': nrm(ks[10], (L, CONF_WIDTH), 0.01),
        'conf_out': nrm(ks[11], (L, CONF_WIDTH, D_MODEL), CONF_WIDTH ** -0.5),
        'ssm_conv_w': nrm(ks[12], (L, SSM_CONV, SSM_XBC), SSM_CONV ** -0.5),
        'ssm_conv_b': nrm(ks[13], (L, SSM_XBC), 0.01),
        'ssm_dt_bias': dt0 + jnp.log(-jnp.expm1(-dt0)),
        'ssm_A_log': jnp.log(jax.random.uniform(ks[15], (L, SSM_HEADS), f32, 1.0, 16.0)),
        'ssm_D': 1.0 + nrm(ks[16], (L, SSM_HEADS), 0.02),
        'ssm_norm_w': 1.0 + nrm(ks[17], (L, SSM_D_INNER), 0.02),
        'ssm_out': nrm(ks[18], (L, SSM_D_INNER, D_MODEL), SSM_D_INNER ** -0.5),
        'attn_out': nrm(ks[19], (L, ATTN_WIDTH, D_MODEL), ATTN_WIDTH ** -0.5),
        'mix_out': nrm(ks[20], (L, D_MODEL, D_MODEL), D_MODEL ** -0.5),
        'ffn_w_gate': nrm(ks[21], (n_dense, D_MODEL, FFN_DENSE), D_MODEL ** -0.5),
        'ffn_w_up': nrm(ks[22], (n_dense, D_MODEL, FFN_DENSE), D_MODEL ** -0.5),
        'ffn_w_down': nrm(ks[23], (n_dense, FFN_DENSE, D_MODEL), FFN_DENSE ** -0.5),
        'moe_router_w': nrm(ks[24], (n_moe, D_MODEL, N_EXPERTS), D_MODEL ** -0.5),
        'moe_router_b': nrm(ks[25], (n_moe, N_EXPERTS), 0.01),
        'moe_w_gate': nrm(ks[26], (n_moe, N_EXPERTS, D_MODEL, FFN_EXPERT), D_MODEL ** -0.5),
        'moe_w_up': nrm(ks[27], (n_moe, N_EXPERTS, D_MODEL, FFN_EXPERT), D_MODEL ** -0.5),
        'moe_w_down': nrm(ks[28], (n_moe, N_EXPERTS, FFN_EXPERT, D_MODEL), FFN_EXPERT ** -0.5),
        'final_norm_w': 1.0 + nrm(ks[29], (D_MODEL,), 0.02),
    }


def reference(x, c, norm1_w, norm2_w, ada_w, ada_b, w_in, conf_conv_w, conf_conv_b, conf_ln_w,
              conf_ln_b, conf_out, ssm_conv_w, ssm_conv_b, ssm_dt_bias, ssm_A_log, ssm_D, ssm_norm_w,
              ssm_out, attn_out, mix_out, ffn_w_gate, ffn_w_up, ffn_w_down, moe_router_w, moe_router_b,
              moe_w_gate, moe_w_up, moe_w_down, final_norm_w):
    c_act = jax.nn.silu(c)
    for l in range(DEPTH):
        mod = c_act @ ada_w[l] + ada_b[l]
        sh1, sc1, g1, sh2, sc2, g2 = [m[:, None, :] for m in jnp.split(mod, 6, axis=-1)]
        h = _rmsnorm(x, norm1_w[l]) * (1.0 + sc1) + sh1
        mix = _token_mixer(h, w_in[l], conf_conv_w[l], conf_conv_b[l], conf_ln_w[l], conf_ln_b[l],
                           conf_out[l], ssm_conv_w[l], ssm_conv_b[l], ssm_dt_bias[l], ssm_A_log[l],
                           ssm_D[l], ssm_norm_w[l], ssm_out[l], attn_out[l], mix_out[l])
        x = x + g1 * mix
        h = _rmsnorm(x, norm2_w[l]) * (1.0 + sc2) + sh2
        if l % 2 == 0:
            j = l // 2
            f = _swiglu(h, ffn_w_gate[j], ffn_w_up[j], ffn_w_down[j])
        else:
            j = l // 2
            f = _moe(h, moe_router_w[j], moe_router_b[j], moe_w_gate[j], moe_w_up[j], moe_w_down[j])
        x = x + g2 * f
    return _rmsnorm(x, final_norm_w)
```

```python
import functools

import jax
import jax.numpy as jnp
from jax import lax
from jax.experimental import pallas as pl
from jax.experimental.pallas import tpu as pltpu

F32 = jnp.float32
BF16 = jnp.bfloat16

D_MODEL = 1024
CONF_WIDTH = 512
CONF_KERNEL = 31
SSM_D_INNER = 1024
SSM_HEADDIM = 64
SSM_HEADS = 16
SSM_GROUPS = 2
SSM_STATE = 128
SSM_CONV = 4
SSM_CHUNK = 128
SSM_XBC = SSM_D_INNER + 2 * SSM_GROUPS * SSM_STATE
ATTN_HEADS = 8
ATTN_HEAD_DIM = 64
ATTN_WIDTH = ATTN_HEADS * ATTN_HEAD_DIM
MOBA_BLOCK = 256
MOBA_TOPK = 3
N_EXPERTS = 8
N_BRANCHES = 3
EPS = 1e-6

LANES = 128
VMEM_LIMIT = 56 * 1024 * 1024
NEG_BIG = -1e30


def _cparams(sem, vmem=VMEM_LIMIT):
    return pltpu.CompilerParams(dimension_semantics=sem, vmem_limit_bytes=vmem)


def _resident(shape):
    zeros = (0,) * len(shape)
    return pl.BlockSpec(shape, lambda *_: zeros, pipeline_mode=pl.Buffered(1))


def _silu(v):
    return v * jax.nn.sigmoid(v)


def _split2(v):
    hi = v.astype(BF16)
    mid = (v - hi.astype(F32)).astype(BF16)
    return hi, mid


def _split3(v):
    hi = v.astype(BF16)
    r = v - hi.astype(F32)
    mid = r.astype(BF16)
    lo = (r - mid.astype(F32)).astype(BF16)
    return hi, mid, lo


def _dot(a, b):
    return jnp.dot(a, b, preferred_element_type=F32)


def _dot_nt(a, b):
    return lax.dot_general(a, b, (((1,), (1,)), ((), ())), preferred_element_type=F32)


def _dot_tn(a, b):
    return lax.dot_general(a, b, (((0,), (0,)), ((), ())), preferred_element_type=F32)


def _ada_kernel(c_ref, w_ref, b_ref, o_ref):
    c = c_ref[...]
    ch, cm = _split2(_silu(c))
    wh, wm = _split2(w_ref[0])
    o_ref[0] = _dot(ch, wh) + _dot(ch, wm) + _dot(cm, wh) + b_ref[0]


def _ada_mod(c, ada_w, ada_b):
    n_layers, d, n = ada_w.shape
    bsz = c.shape[0]
    rows = 8
    tn = 1536
    cpad = jnp.zeros((rows, d), F32).at[:bsz].set(c)
    out = pl.pallas_call(
        _ada_kernel,
        out_shape=jax.ShapeDtypeStruct((n_layers, rows, n), F32),
        grid=(n_layers, n // tn),
        in_specs=[
            pl.BlockSpec((rows, d), lambda l, j: (0, 0)),
            pl.BlockSpec((1, d, tn), lambda l, j: (l, 0, j)),
            pl.BlockSpec((1, 1, tn), lambda l, j: (l, 0, j)),
        ],
        out_specs=pl.BlockSpec((1, rows, tn), lambda l, j: (l, 0, j)),
        compiler_params=_cparams(("parallel", "parallel")),
        name="ada_mod",
    )(cpad, ada_w, ada_b.reshape(n_layers, 1, n))
    return out[:, :bsz]


def _modulated_norm(x, nw, shift, scale):
    ms = jnp.mean(x * x, axis=-1, keepdims=True)
    return (x * lax.rsqrt(ms + EPS) * nw) * (1.0 + scale) + shift


def _inproj_kernel(x_ref, nw_ref, mod_ref, wc_ref, wz_ref, wx_ref, wd_ref, wq_ref, wg_ref,
                   oc_ref, oz_ref, ox_ref, od_ref, oq_ref, og_ref):
    h = _modulated_norm(x_ref[...], nw_ref[...], mod_ref[0, 0:1, :], mod_ref[0, 1:2, :])
    hb = h.astype(BF16)
    oc_ref[...] = _dot(hb, wc_ref[...]).astype(BF16)
    oz_ref[...] = _dot(hb, wz_ref[...]).astype(BF16)
    ox_ref[...] = _dot(hb, wx_ref[...]).astype(BF16)
    od_ref[...] = _dot(hb, wd_ref[...])
    oq_ref[...] = _dot(hb, wq_ref[...]).astype(BF16)
    og_ref[...] = _dot(hb, wg_ref[...]).astype(BF16)


def _inproj(x2, nw, mod_l, ws, seq, tm=512):
    t, d = x2.shape
    per_b = seq // tm
    widths = [w.shape[1] for w in ws]
    dts = [BF16, BF16, BF16, F32, BF16, BF16]
    return pl.pallas_call(
        _inproj_kernel,
        out_shape=[jax.ShapeDtypeStruct((t, wd), dt) for wd, dt in zip(widths, dts)],
        grid=(t // tm,),
        in_specs=[
            pl.BlockSpec((tm, d), lambda i: (i, 0)),
            _resident((1, d)),
            pl.BlockSpec((1, 6, d), lambda i: (i // per_b, 0, 0)),
        ] + [_resident(w.shape) for w in ws],
        out_specs=[pl.BlockSpec((tm, wd), lambda i: (i, 0)) for wd in widths],
        compiler_params=_cparams(("parallel",)),
        name="inproj",
    )(x2, nw, mod_l, *ws)


_CONF_HALO = 32
_CONF_ROWS = 32


def _conf_kernel(cin_ref, w_ref, b_ref, lnw_ref, lnb_ref, o_ref, xg_ref, *, ts):
    s = pl.program_id(1)

    @pl.when(s == 0)
    def _():
        xg_ref[0:_CONF_HALO, :] = jnp.zeros((_CONF_HALO, CONF_WIDTH), F32)

    @pl.when(s > 0)
    def _():
        xg_ref[0:_CONF_HALO, :] = xg_ref[ts:ts + _CONF_HALO, :]

    cin = cin_ref[...].astype(F32)
    xg_ref[_CONF_HALO:_CONF_HALO + ts, :] = cin[:, :CONF_WIDTH] * jax.nn.sigmoid(cin[:, CONF_WIDTH:])
    first = _CONF_HALO - (CONF_KERNEL - 1)
    for r0 in range(0, ts, _CONF_ROWS):
        acc = jnp.broadcast_to(b_ref[...], (_CONF_ROWS, CONF_WIDTH))
        for k in range(CONF_KERNEL):
            start = r0 + first + k
            acc = acc + w_ref[k:k + 1, :] * xg_ref[start:start + _CONF_ROWS, :]
        mu = jnp.mean(acc, axis=-1, keepdims=True)
        xc = acc - mu
        var = jnp.mean(xc * xc, axis=-1, keepdims=True)
        y = xc * lax.rsqrt(var + EPS) * lnw_ref[...] + lnb_ref[...]
        o_ref[r0:r0 + _CONF_ROWS, :] = _silu(y).astype(BF16)


def _conformer(conf_in, cw, cb, lnw, lnb, bsz, seq, ts=256):
    t = conf_in.shape[0]
    per_b = seq // ts
    return pl.pallas_call(
        functools.partial(_conf_kernel, ts=ts),
        out_shape=jax.ShapeDtypeStruct((t, CONF_WIDTH), BF16),
        grid=(bsz, per_b),
        in_specs=[
            pl.BlockSpec((ts, 2 * CONF_WIDTH), lambda b, s: (b * per_b + s, 0)),
            _resident(cw.shape), _resident(cb.shape), _resident(lnw.shape), _resident(lnb.shape),
        ],
        out_specs=pl.BlockSpec((ts, CONF_WIDTH), lambda b, s: (b * per_b + s, 0)),
        scratch_shapes=[pltpu.VMEM((ts + _CONF_HALO, CONF_WIDTH), F32)],
        compiler_params=_cparams(("parallel", "arbitrary")),
        name="conformer",
    )(conf_in, cw, cb, lnw, lnb)


_SSM_HALO = 8
_GROUP_W = SSM_D_INNER // SSM_GROUPS


def _ssd_kernel(xbc_ref, z_ref, dt_ref, cw_ref, cb_ref, dtb_ref, alog_ref, dfull_ref, nw_ref,
                e3_ref, o_ref, xh_ref, st_ref):
    ch = SSM_CHUNK
    s = pl.program_id(1)

    @pl.when(s == 0)
    def _():
        xh_ref[0:_SSM_HALO, :] = jnp.zeros((_SSM_HALO, SSM_XBC), F32)
        st_ref[...] = jnp.zeros(st_ref.shape, F32)

    @pl.when(s > 0)
    def _():
        xh_ref[0:_SSM_HALO, :] = xh_ref[ch:ch + _SSM_HALO, :]

    xh_ref[_SSM_HALO:_SSM_HALO + ch, :] = xbc_ref[...].astype(F32)
    first = _SSM_HALO - (SSM_CONV - 1)
    acc = jnp.broadcast_to(cb_ref[...], (ch, SSM_XBC))
    for k in range(SSM_CONV):
        acc = acc + cw_ref[k:k + 1, :] * xh_ref[first + k:first + k + ch, :]
    xc = _silu(acc)
    xs = xc[:, :SSM_D_INNER]
    bm = xc[:, SSM_D_INNER:SSM_D_INNER + SSM_GROUPS * SSM_STATE].astype(BF16)
    cm = xc[:, SSM_D_INNER + SSM_GROUPS * SSM_STATE:].astype(BF16)

    dtr = dt_ref[...] + dtb_ref[...]
    dt = jnp.maximum(dtr, 0.0) + jnp.log(1.0 + jnp.exp(-jnp.abs(dtr)))
    ad = dt * (-jnp.exp(alog_ref[...]))

    row = lax.broadcasted_iota(jnp.int32, (ch, ch), 0)
    col = lax.broadcasted_iota(jnp.int32, (ch, ch), 1)
    causal = row >= col
    tril = jnp.where(causal, 1.0, 0.0).astype(BF16)
    a3 = _dot(tril, jnp.concatenate(_split3(ad), axis=1))
    acum = a3[:, :LANES] + a3[:, LANES:2 * LANES] + a3[:, 2 * LANES:]
    acum_t = acum.T

    dh, dm = _split2(dt)
    lhs = jnp.concatenate([
        jnp.concatenate(_split3(acum), axis=1),
        jnp.concatenate([dh, dm, jnp.zeros_like(dh)], axis=1)], axis=0)
    ex = _dot(lhs, e3_ref[...])
    acum_f = ex[:ch]
    dt_f = ex[ch:]
    xdt = xs * dt_f
    last = acum_f[ch - 1:ch, :]
    xdt_b = xdt.astype(BF16)
    xd_b = (xdt * jnp.exp(last - acum_f)).astype(BF16)
    eacum = jnp.exp(acum_f)
    cdecay = jnp.exp(last)

    lane = lax.broadcasted_iota(jnp.int32, (ch, LANES), 1)
    ys = []
    for g in range(SSM_GROUPS):
        bg = bm[:, g * SSM_STATE:(g + 1) * SSM_STATE]
        cg = cm[:, g * SSM_STATE:(g + 1) * SSM_STATE]
        cb = _dot_nt(cg, bg)
        parts = []
        for pp in range(_GROUP_W // LANES):
            ms = []
            for hh in range(2):
                h = g * (SSM_HEADS // SSM_GROUPS) + 2 * pp + hh
                seg = acum[:, h:h + 1] - acum_t[h:h + 1, :]
                ms.append((cb * jnp.where(causal, jnp.exp(seg), 0.0)).astype(BF16))
            c0 = g * _GROUP_W + pp * LANES
            xp = xdt_b[:, c0:c0 + LANES]
            x2 = jnp.concatenate([jnp.where(lane < SSM_HEADDIM, xp, 0.0).astype(BF16),
                                  jnp.where(lane >= SSM_HEADDIM, xp, 0.0).astype(BF16)], axis=0)
            parts.append(_dot(jnp.concatenate(ms, axis=1), x2))
        gs = slice(g * _GROUP_W, (g + 1) * _GROUP_W)
        st = st_ref[g]
        y_off = _dot(cg, st.astype(BF16)) * eacum[:, gs]
        st_ref[g] = st * cdecay[:, gs] + _dot_tn(bg, xd_b[:, gs])
        ys.append(jnp.concatenate(parts, axis=1) + y_off)
    y = jnp.concatenate(ys, axis=1) + xs * dfull_ref[...]

    gz = y * _silu(z_ref[...].astype(F32))
    outs = []
    for g in range(SSM_GROUPS):
        gg = gz[:, g * _GROUP_W:(g + 1) * _GROUP_W]
        outs.append(gg * lax.rsqrt(jnp.mean(gg * gg, axis=-1, keepdims=True) + EPS))
    o_ref[...] = (jnp.concatenate(outs, axis=1) * nw_ref[...]).astype(BF16)


def _ssd(xbc, z, dt, cw, cb, dtb, alog, dfull, nw, e3, bsz, seq):
    t = xbc.shape[0]
    ch = SSM_CHUNK
    per_b = seq // ch
    tile = lambda w: pl.BlockSpec((ch, w), lambda b, s: (b * per_b + s, 0))
    return pl.pallas_call(
        _ssd_kernel,
        out_shape=jax.ShapeDtypeStruct((t, SSM_D_INNER), BF16),
        grid=(bsz, per_b),
        in_specs=[tile(SSM_XBC), tile(SSM_D_INNER), tile(LANES)]
        + [_resident(a.shape) for a in (cw, cb, dtb, alog, dfull, nw, e3)],
        out_specs=tile(SSM_D_INNER),
        scratch_shapes=[pltpu.VMEM((ch + _SSM_HALO, SSM_XBC), F32),
                        pltpu.VMEM((SSM_GROUPS, SSM_STATE, _GROUP_W), F32)],
        compiler_params=_cparams(("parallel", "arbitrary")),
        name="ssd",
    )(xbc, z, dt, cw, cb, dtb, alog, dfull, nw, e3)


_VA_ROWS = 80


def _moba_prep_kernel(q_ref, k_ref, v_ref, qa_ref, ka_ref, va_ref, km_ref, *, seq):
    nb = seq // MOBA_BLOCK
    hd = ATTN_HEAD_DIM
    q = q_ref[...]
    k = k_ref[...].astype(F32)
    for j in range(nb):
        km_ref[j:j + 1, :] = jnp.mean(k[j * MOBA_BLOCK:(j + 1) * MOBA_BLOCK, :], axis=0, keepdims=True)
    km = km_ref[...]
    qt = q.astype(F32).T * (hd ** -0.5)
    vt = v_ref[...].astype(F32).T

    shift = MOBA_BLOCK.bit_length() - 1
    lane_k = lax.broadcasted_iota(jnp.int32, (seq, LANES), 1)
    key_blk = lax.shift_right_logical(lax.broadcasted_iota(jnp.int32, (seq, LANES), 0), shift)
    lane_m = lax.broadcasted_iota(jnp.int32, (nb, LANES), 1)
    sub = lax.broadcasted_iota(jnp.int32, (nb, seq), 0).astype(F32)
    qblk = lax.shift_right_logical(lax.broadcasted_iota(jnp.int32, (1, seq), 1), shift).astype(F32)
    pad_rows = jnp.zeros((LANES - hd - nb, seq), BF16)
    ones_row = jnp.where(lax.broadcasted_iota(jnp.int32, (_VA_ROWS - hd, seq), 0) == 0, 1.0, 0.0).astype(BF16)

    for hh in range(2):
        lo = hh * hd
        in_head = (lane_m >= lo) & (lane_m < lo + hd)
        kmh = jnp.where(in_head, km, 0.0)
        g3 = _dot_nt(jnp.concatenate(_split3(kmh), axis=0), q)
        gate = g3[:nb] + g3[nb:2 * nb] + g3[2 * nb:]
        gate = jnp.where(sub < qblk, gate, -jnp.inf)
        sel = sub == qblk
        for r in range(MOBA_TOPK):
            mx = jnp.max(gate, axis=0, keepdims=True)
            idx = jnp.min(jnp.where(gate == mx, sub, float(nb)), axis=0, keepdims=True)
            hit = sub == idx
            sel = sel | (hit & (qblk > r))
            gate = jnp.where(hit, -jnp.inf, gate)
        bias = jnp.where(sel, 0.0, NEG_BIG).astype(BF16)
        qh = qt[lo:lo + hd, :].astype(BF16)
        if hh == 0:
            qa_ref[0, hh] = jnp.concatenate([qh, bias, pad_rows], axis=0)
            onehot = (lane_k - hd) == key_blk
            ka_ref[0, hh] = jnp.where(lane_k < hd, k, jnp.where(onehot, 1.0, 0.0)).astype(BF16)
        else:
            qa_ref[0, hh] = jnp.concatenate([bias, pad_rows, qh], axis=0)
            onehot = lane_k == key_blk
            ka_ref[0, hh] = jnp.where(lane_k >= hd, k, jnp.where(onehot, 1.0, 0.0)).astype(BF16)
        va_ref[0, hh] = jnp.concatenate([vt[lo:lo + hd, :].astype(BF16), ones_row], axis=0)


def _moba_prep(qkv, bsz, seq):
    pairs = ATTN_HEADS // 2
    nb = seq // MOBA_BLOCK
    assert 2 * nb <= LANES - ATTN_HEAD_DIM
    col = lambda off: pl.BlockSpec((seq, LANES), lambda b, p: (b, off + p))
    return pl.pallas_call(
        functools.partial(_moba_prep_kernel, seq=seq),
        out_shape=[jax.ShapeDtypeStruct((bsz, ATTN_HEADS, LANES, seq), BF16),
                   jax.ShapeDtypeStruct((bsz, ATTN_HEADS, seq, LANES), BF16),
                   jax.ShapeDtypeStruct((bsz, ATTN_HEADS, _VA_ROWS, seq), BF16)],
        grid=(bsz, pairs),
        in_specs=[col(0), col(pairs), col(2 * pairs)],
        out_specs=[pl.BlockSpec((1, 2, LANES, seq), lambda b, p: (b, p, 0, 0)),
                   pl.BlockSpec((1, 2, seq, LANES), lambda b, p: (b, p, 0, 0)),
                   pl.BlockSpec((1, 2, _VA_ROWS, seq), lambda b, p: (b, p, 0, 0))],
        scratch_shapes=[pltpu.VMEM((nb, LANES), F32)],
        compiler_params=_cparams(("parallel", "parallel")),
        name="moba_prep",
    )(qkv, qkv, qkv)


def _moba_attn_kernel(qa_ref, ka_ref, va_ref, o_ref):
    i = pl.program_id(2)
    tq = MOBA_BLOCK
    hd = ATTN_HEAD_DIM
    krow = lax.broadcasted_iota(jnp.int32, (tq, tq), 0)
    qcol = lax.broadcasted_iota(jnp.int32, (tq, tq), 1)
    q = [qa_ref[0, hh] for hh in range(2)]

    def scores(hh, j):
        start = pl.multiple_of(j * tq, tq)
        kj = ka_ref[0, hh, pl.ds(start, tq), :]
        vj = va_ref[0, hh, :, pl.ds(start, tq)]
        return _dot(kj, q[hh]), vj

    state = []
    for hh in range(2):
        st, vj = scores(hh, i)
        st = jnp.where(krow <= qcol, st, NEG_BIG)
        m = jnp.max(st, axis=0, keepdims=True)
        p = jnp.exp(st - m)
        state += [m, _dot(vj, p.astype(BF16))]

    def body(j, carry):
        out = []
        for hh in range(2):
            m, acc = carry[2 * hh], carry[2 * hh + 1]
            st, vj = scores(hh, j)
            m_new = jnp.maximum(m, jnp.max(st, axis=0, keepdims=True))
            p = jnp.exp(st - m_new)
            out += [m_new, acc * jnp.exp(m - m_new) + _dot(vj, p.astype(BF16))]
        return tuple(out)

    state = lax.fori_loop(0, i, body, tuple(state))
    ot = jnp.concatenate([state[2 * hh + 1][:hd] / state[2 * hh + 1][hd:hd + 1] for hh in range(2)], axis=0)
    o_ref[...] = ot.T.astype(BF16)


def _moba_attn(qa, ka, va, bsz, seq):
    pairs = ATTN_HEADS // 2
    nq = seq // MOBA_BLOCK
    return pl.pallas_call(
        _moba_attn_kernel,
        out_shape=jax.ShapeDtypeStruct((bsz * seq, ATTN_WIDTH), BF16),
        grid=(bsz, pairs, nq),
        in_specs=[pl.BlockSpec((1, 2, LANES, MOBA_BLOCK), lambda b, p, i: (b, p, 0, i)),
                  pl.BlockSpec((1, 2, seq, LANES), lambda b, p, i: (b, p, 0, 0)),
                  pl.BlockSpec((1, 2, _VA_ROWS, seq), lambda b, p, i: (b, p, 0, 0))],
        out_specs=pl.BlockSpec((MOBA_BLOCK, LANES), lambda b, p, i: (b * nq + i, p)),
        compiler_params=_cparams(("parallel", "parallel", "arbitrary")),
        name="moba_attn",
    )(qa, ka, va)


def _mix_kernel(x_ref, u_ref, y_ref, a_ref, g_ref, mod_ref, wc_ref, ws_ref, wa_ref, wm_ref, o_ref):
    d = D_MODEL
    out_a = _dot(u_ref[...], wc_ref[...])
    out_b = _dot(y_ref[...], ws_ref[...])
    out_c = _dot(a_ref[...], wa_ref[...])
    gt = jax.nn.sigmoid(g_ref[...].astype(F32))
    merged = gt[:, :d] * out_a + gt[:, d:2 * d] * out_b + gt[:, 2 * d:] * out_c
    o_ref[...] = x_ref[...] + mod_ref[0, 2:3, :] * _dot(merged.astype(BF16), wm_ref[...])


def _mix(x2, u, y, a, gate_in, mod_l, wc, ws, wa, wm, seq, tm=512):
    t, d = x2.shape
    per_b = seq // tm
    tile = lambda w: pl.BlockSpec((tm, w), lambda i: (i, 0))
    return pl.pallas_call(
        _mix_kernel,
        out_shape=jax.ShapeDtypeStruct((t, d), F32),
        grid=(t // tm,),
        in_specs=[tile(d), tile(CONF_WIDTH), tile(SSM_D_INNER), tile(ATTN_WIDTH), tile(N_BRANCHES * d),
                  pl.BlockSpec((1, 6, d), lambda i: (i // per_b, 0, 0)),
                  _resident(wc.shape), _resident(ws.shape), _resident(wa.shape), _resident(wm.shape)],
        out_specs=tile(d),
        compiler_params=_cparams(("parallel",)),
        name="mix_out",
    )(x2, u, y, a, gate_in, mod_l, wc, ws, wa, wm)


def _ffn_kernel(x_ref, nw_ref, mod_ref, rw_ref, rb_ref, wg_ref, wu_ref, wd_ref, fw_ref, o_ref,
                hb_ref, acc_ref, gates_ref, *, n_experts, final_norm):
    e = pl.program_id(1)
    f = pl.program_id(2)
    first = (e == 0) & (f == 0)
    last = (e == pl.num_programs(1) - 1) & (f == pl.num_programs(2) - 1)

    @pl.when(first)
    def _():
        h = _modulated_norm(x_ref[...], nw_ref[...], mod_ref[0, 3:4, :], mod_ref[0, 4:5, :])
        hb = h.astype(BF16)
        hb_ref[...] = hb
        acc_ref[...] = jnp.zeros(acc_ref.shape, F32)
        if n_experts > 1:
            l3 = _dot(hb, rw_ref[...])
            logits = l3[:, :LANES] + l3[:, LANES:2 * LANES] + l3[:, 2 * LANES:] + rb_ref[...]
            lane = lax.broadcasted_iota(jnp.int32, logits.shape, 1)
            lg = jnp.where(lane < n_experts, logits, -jnp.inf)
            m1 = jnp.max(lg, axis=-1, keepdims=True)
            i1 = jnp.min(jnp.where(lg == m1, lane, LANES), axis=-1, keepdims=True)
            lg2 = jnp.where(lane == i1, -jnp.inf, lg)
            m2 = jnp.max(lg2, axis=-1, keepdims=True)
            i2 = jnp.min(jnp.where(lg2 == m2, lane, LANES), axis=-1, keepdims=True)
            e2 = jnp.exp(m2 - m1)
            den = 1.0 + e2
            gates_ref[...] = jnp.where(lane == i1, 1.0 / den, 0.0) + jnp.where(lane == i2, e2 / den, 0.0)

    hb = hb_ref[...]
    act = (_silu(_dot(hb, wg_ref[0])) * _dot(hb, wu_ref[0])).astype(BF16)
    contrib = _dot(act, wd_ref[0])
    if n_experts > 1:
        lane = lax.broadcasted_iota(jnp.int32, gates_ref.shape, 1)
        gcol = jnp.sum(jnp.where(lane == e, gates_ref[...], 0.0), axis=-1, keepdims=True)
        contrib = gcol * contrib
    acc_ref[...] += contrib

    @pl.when(last)
    def _():
        y = x_ref[...] + mod_ref[0, 5:6, :] * acc_ref[...]
        if final_norm:
            y = y * lax.rsqrt(jnp.mean(y * y, axis=-1, keepdims=True) + EPS) * fw_ref[...]
        o_ref[...] = y


def _ffn(x2, nw, mod_l, rw3, rb, wg, wu, wd, fw, seq, *, tm, tf, final_norm):
    t, d = x2.shape
    n_experts, _, ff = wg.shape
    per_b = seq // tm
    return pl.pallas_call(
        functools.partial(_ffn_kernel, n_experts=n_experts, final_norm=final_norm),
        out_shape=jax.ShapeDtypeStruct((t, d), F32),
        grid=(t // tm, n_experts, ff // tf),
        in_specs=[pl.BlockSpec((tm, d), lambda i, e, f: (i, 0)),
                  _resident((1, d)),
                  pl.BlockSpec((1, 6, d), lambda i, e, f: (i // per_b, 0, 0)),
                  _resident(rw3.shape), _resident(rb.shape),
                  pl.BlockSpec((1, d, tf), lambda i, e, f: (e, 0, f)),
                  pl.BlockSpec((1, d, tf), lambda i, e, f: (e, 0, f)),
                  pl.BlockSpec((1, tf, d), lambda i, e, f: (e, f, 0)),
                  _resident((1, d))],
        out_specs=pl.BlockSpec((tm, d), lambda i, e, f: (i, 0)),
        scratch_shapes=[pltpu.VMEM((tm, d), BF16), pltpu.VMEM((tm, d), F32), pltpu.VMEM((tm, LANES), F32)],
        compiler_params=_cparams(("parallel", "arbitrary", "arbitrary")),
        name="ffn_moe" if n_experts > 1 else "ffn_dense",
    )(x2, nw, mod_l, rw3, rb, wg, wu, wd, fw)


def _pad_lanes(v):
    return jnp.zeros((1, LANES), F32).at[0, :v.shape[0]].set(v)


def kernel(x, c, norm1_w, norm2_w, ada_w, ada_b, w_in, conf_conv_w, conf_conv_b, conf_ln_w, conf_ln_b,
           conf_out, ssm_conv_w, ssm_conv_b, ssm_dt_bias, ssm_A_log, ssm_D, ssm_norm_w, ssm_out, attn_out,
           mix_out, ffn_w_gate, ffn_w_up, ffn_w_down, moe_router_w, moe_router_b, moe_w_gate, moe_w_up,
           moe_w_down, final_norm_w):
    bsz, seq, d = x.shape
    depth = w_in.shape[0]
    t = bsz * seq
    row = lambda v: v.reshape(1, -1).astype(F32)

    mod = _ada_mod(c, ada_w, ada_b).reshape(depth, bsz, 6, d)
    head_of = jnp.arange(SSM_D_INNER) // SSM_HEADDIM
    e1 = (jnp.arange(LANES)[:, None] == head_of[None, :]).astype(BF16)
    e3 = jnp.concatenate([e1, e1, e1], axis=0)
    zero_router = (jnp.zeros((d, 3 * LANES), BF16), jnp.zeros((1, LANES), F32))

    x2 = x.reshape(t, d)
    edges = [0]
    for wdt in (2 * CONF_WIDTH, SSM_D_INNER, SSM_XBC, SSM_HEADS, 3 * ATTN_WIDTH, N_BRANCHES * D_MODEL):
        edges.append(edges[-1] + wdt)
    for l in range(depth):
        wl = w_in[l]
        parts = [wl[:, edges[i]:edges[i + 1]] for i in range(6)]
        parts[3] = jnp.zeros((d, LANES), F32).at[:, :SSM_HEADS].set(parts[3])
        ws = [p.astype(BF16) for p in parts]
        conf_in, z, xbc, dt, qkv, gate_in = _inproj(x2, row(norm1_w[l]), mod[l], ws, seq)

        u = _conformer(conf_in, conf_conv_w[l], row(conf_conv_b[l]), row(conf_ln_w[l]), row(conf_ln_b[l]),
                       bsz, seq)
        y = _ssd(xbc, z, dt, ssm_conv_w[l], row(ssm_conv_b[l]), _pad_lanes(ssm_dt_bias[l]),
                 _pad_lanes(ssm_A_log[l]), row(jnp.repeat(ssm_D[l], SSM_HEADDIM)), row(ssm_norm_w[l]), e3,
                 bsz, seq)
        qa, ka, va = _moba_prep(qkv, bsz, seq)
        a = _moba_attn(qa, ka, va, bsz, seq)
        x2 = _mix(x2, u, y, a, gate_in, mod[l], conf_out[l].astype(BF16), ssm_out[l].astype(BF16),
                  attn_out[l].astype(BF16), mix_out[l].astype(BF16), seq)

        final = l == depth - 1
        j = l // 2
        if l % 2 == 0:
            x2 = _ffn(x2, row(norm2_w[l]), mod[l], *zero_router, ffn_w_gate[j][None].astype(BF16),
                      ffn_w_up[j][None].astype(BF16), ffn_w_down[j][None].astype(BF16), row(final_norm_w),
                      seq, tm=512, tf=1408, final_norm=final)
        else:
            rw = jnp.zeros((d, LANES), F32).at[:, :N_EXPERTS].set(moe_router_w[j])
            rw3 = jnp.concatenate(_split3(rw), axis=1)
            x2 = _ffn(x2, row(norm2_w[l]), mod[l], rw3, _pad_lanes(moe_router_b[j]),
                      moe_w_gate[j].astype(BF16), moe_w_up[j].astype(BF16), moe_w_down[j].astype(BF16),
                      row(final_norm_w), seq, tm=1024, tf=896, final_norm=final)
    return x2.reshape(bsz, seq, d)
```

```python
import functools

import jax
import jax.numpy as jnp
from jax import lax
from jax.experimental import pallas as pl
from jax.experimental.pallas import tpu as pltpu

F32 = jnp.float32
BF16 = jnp.bfloat16

D_MODEL = 1024
CONF_WIDTH = 512
CONF_KERNEL = 31
SSM_D_INNER = 1024
SSM_HEADDIM = 64
SSM_HEADS = 16
SSM_GROUPS = 2
SSM_STATE = 128
SSM_CONV = 4
SSM_CHUNK = 128
SSM_XBC = SSM_D_INNER + 2 * SSM_GROUPS * SSM_STATE
ATTN_HEADS = 8
ATTN_HEAD_DIM = 64
ATTN_WIDTH = ATTN_HEADS * ATTN_HEAD_DIM
MOBA_BLOCK = 256
MOBA_TOPK = 3
N_EXPERTS = 8
N_BRANCHES = 3
EPS = 1e-6

LANES = 128
VMEM_LIMIT = 56 * 1024 * 1024
NEG_BIG = -1e30


def _cparams(sem, vmem=VMEM_LIMIT):
    return pltpu.CompilerParams(dimension_semantics=sem, vmem_limit_bytes=vmem)


def _resident(shape):
    zeros = (0,) * len(shape)
    return pl.BlockSpec(shape, lambda *_: zeros, pipeline_mode=pl.Buffered(1))


def _silu(v):
    return v * jax.nn.sigmoid(v)


def _split2(v):
    hi = v.astype(BF16)
    mid = (v - hi.astype(F32)).astype(BF16)
    return hi, mid


def _split3(v):
    hi = v.astype(BF16)
    r = v - hi.astype(F32)
    mid = r.astype(BF16)
    lo = (r - mid.astype(F32)).astype(BF16)
    return hi, mid, lo


def _dot(a, b):
    return jnp.dot(a, b, preferred_element_type=F32)


def _dot_nt(a, b):
    return lax.dot_general(a, b, (((1,), (1,)), ((), ())), preferred_element_type=F32)


def _dot_tn(a, b):
    return lax.dot_general(a, b, (((0,), (0,)), ((), ())), preferred_element_type=F32)


def _ada_kernel(c_ref, w_ref, b_ref, o_ref):
    c = c_ref[...]
    ch, cm = _split2(_silu(c))
    wh, wm = _split2(w_ref[0])
    o_ref[0] = _dot(ch, wh) + _dot(ch, wm) + _dot(cm, wh) + b_ref[0]


def _ada_mod(c, ada_w, ada_b):
    n_layers, d, n = ada_w.shape
    bsz = c.shape[0]
    rows = 8
    tn = 1536
    cpad = jnp.zeros((rows, d), F32).at[:bsz].set(c)
    out = pl.pallas_call(
        _ada_kernel,
        out_shape=jax.ShapeDtypeStruct((n_layers, rows, n), F32),
        grid=(n_layers, n // tn),
        in_specs=[
            pl.BlockSpec((rows, d), lambda l, j: (0, 0)),
            pl.BlockSpec((1, d, tn), lambda l, j: (l, 0, j)),
            pl.BlockSpec((1, 1, tn), lambda l, j: (l, 0, j)),
        ],
        out_specs=pl.BlockSpec((1, rows, tn), lambda l, j: (l, 0, j)),
        compiler_params=_cparams(("parallel", "parallel")),
        name="ada_mod",
    )(cpad, ada_w, ada_b.reshape(n_layers, 1, n))
    return out[:, :bsz]


def _modulated_norm(x, nw, shift, scale):
    ms = jnp.mean(x * x, axis=-1, keepdims=True)
    return (x * lax.rsqrt(ms + EPS) * nw) * (1.0 + scale) + shift


def _inproj_kernel(x_ref, nw_ref, mod_ref, wc_ref, wz_ref, wx_ref, wd_ref, wq_ref, wg_ref,
                   oc_ref, oz_ref, ox_ref, od_ref, oq_ref, og_ref):
    h = _modulated_norm(x_ref[...], nw_ref[...], mod_ref[0, 0:1, :], mod_ref[0, 1:2, :])
    hb = h.astype(BF16)
    oc_ref[...] = _dot(hb, wc_ref[...]).astype(BF16)
    oz_ref[...] = _dot(hb, wz_ref[...]).astype(BF16)
    ox_ref[...] = _dot(hb, wx_ref[...]).astype(BF16)
    od_ref[...] = _dot(hb, wd_ref[...])
    oq_ref[...] = _dot(hb, wq_ref[...]).astype(BF16)
    og_ref[...] = _dot(hb, wg_ref[...]).astype(BF16)


def _inproj(x2, nw, mod_l, ws, seq, tm=512):
    t, d = x2.shape
    per_b = seq // tm
    widths = [w.shape[1] for w in ws]
    dts = [BF16, BF16, BF16, F32, BF16, BF16]
    return pl.pallas_call(
        _inproj_kernel,
        out_shape=[jax.ShapeDtypeStruct((t, wd), dt) for wd, dt in zip(widths, dts)],
        grid=(t // tm,),
        in_specs=[
            pl.BlockSpec((tm, d), lambda i: (i, 0)),
            _resident((1, d)),
            pl.BlockSpec((1, 6, d), lambda i: (i // per_b, 0, 0)),
        ] + [_resident(w.shape) for w in ws],
        out_specs=[pl.BlockSpec((tm, wd), lambda i: (i, 0)) for wd in widths],
        compiler_params=_cparams(("parallel",)),
        name="inproj",
    )(x2, nw, mod_l, *ws)


_CONF_HALO = 32
_CONF_ROWS = 32


def _conf_kernel(cin_ref, w_ref, b_ref, lnw_ref, lnb_ref, o_ref, xg_ref, *, ts):
    s = pl.program_id(1)

    @pl.when(s == 0)
    def _():
        xg_ref[0:_CONF_HALO, :] = jnp.zeros((_CONF_HALO, CONF_WIDTH), F32)

    @pl.when(s > 0)
    def _():
        xg_ref[0:_CONF_HALO, :] = xg_ref[ts:ts + _CONF_HALO, :]

    cin = cin_ref[...].astype(F32)
    xg_ref[_CONF_HALO:_CONF_HALO + ts, :] = cin[:, :CONF_WIDTH] * jax.nn.sigmoid(cin[:, CONF_WIDTH:])
    first = _CONF_HALO - (CONF_KERNEL - 1)
    for r0 in range(0, ts, _CONF_ROWS):
        acc = jnp.broadcast_to(b_ref[...], (_CONF_ROWS, CONF_WIDTH))
        for k in range(CONF_KERNEL):
            start = r0 + first + k
            acc = acc + w_ref[k:k + 1, :] * xg_ref[start:start + _CONF_ROWS, :]
        mu = jnp.mean(acc, axis=-1, keepdims=True)
        xc = acc - mu
        var = jnp.mean(xc * xc, axis=-1, keepdims=True)
        y = xc * lax.rsqrt(var + EPS) * lnw_ref[...] + lnb_ref[...]
        o_ref[r0:r0 + _CONF_ROWS, :] = _silu(y).astype(BF16)


def _conformer(conf_in, cw, cb, lnw, lnb, bsz, seq, ts=256):
    t = conf_in.shape[0]
    per_b = seq // ts
    return pl.pallas_call(
        functools.partial(_conf_kernel, ts=ts),
        out_shape=jax.ShapeDtypeStruct((t, CONF_WIDTH), BF16),
        grid=(bsz, per_b),
        in_specs=[
            pl.BlockSpec((ts, 2 * CONF_WIDTH), lambda b, s: (b * per_b + s, 0)),
            _resident(cw.shape), _resident(cb.shape), _resident(lnw.shape), _resident(lnb.shape),
        ],
        out_specs=pl.BlockSpec((ts, CONF_WIDTH), lambda b, s: (b * per_b + s, 0)),
        scratch_shapes=[pltpu.VMEM((ts + _CONF_HALO, CONF_WIDTH), F32)],
        compiler_params=_cparams(("parallel", "arbitrary")),
        name="conformer",
    )(conf_in, cw, cb, lnw, lnb)


_SSM_HALO = 8
_GROUP_W = SSM_D_INNER // SSM_GROUPS


def _ssd_kernel(xbc_ref, z_ref, dt_ref, cw_ref, cb_ref, dtb_ref, alog_ref, dfull_ref, nw_ref,
                e3_ref, o_ref, xh_ref, st_ref):
    ch = SSM_CHUNK
    s = pl.program_id(1)

    @pl.when(s == 0)
    def _():
        xh_ref[0:_SSM_HALO, :] = jnp.zeros((_SSM_HALO, SSM_XBC), F32)
        st_ref[...] = jnp.zeros(st_ref.shape, F32)

    @pl.when(s > 0)
    def _():
        xh_ref[0:_SSM_HALO, :] = xh_ref[ch:ch + _SSM_HALO, :]

    xh_ref[_SSM_HALO:_SSM_HALO + ch, :] = xbc_ref[...].astype(F32)
    first = _SSM_HALO - (SSM_CONV - 1)
    acc = jnp.broadcast_to(cb_ref[...], (ch, SSM_XBC))
    for k in range(SSM_CONV):
        acc = acc + cw_ref[k:k + 1, :] * xh_ref[first + k:first + k + ch, :]
    xc = _silu(acc)
    xs = xc[:, :SSM_D_INNER]
    bm = xc[:, SSM_D_INNER:SSM_D_INNER + SSM_GROUPS * SSM_STATE].astype(BF16)
    cm = xc[:, SSM_D_INNER + SSM_GROUPS * SSM_STATE:].astype(BF16)

    dtr = dt_ref[...] + dtb_ref[...]
    dt = jnp.maximum(dtr, 0.0) + jnp.log(1.0 + jnp.exp(-jnp.abs(dtr)))
    ad = dt * (-jnp.exp(alog_ref[...]))

    row = lax.broadcasted_iota(jnp.int32, (ch, ch), 0)
    col = lax.broadcasted_iota(jnp.int32, (ch, ch), 1)
    causal = row >= col
    tril = jnp.where(causal, 1.0, 0.0).astype(BF16)
    a3 = _dot(tril, jnp.concatenate(_split3(ad), axis=1))
    acum = a3[:, :LANES] + a3[:, LANES:2 * LANES] + a3[:, 2 * LANES:]
    acum_t = acum.T

    dh, dm = _split2(dt)
    lhs = jnp.concatenate([
        jnp.concatenate(_split3(acum), axis=1),
        jnp.concatenate([dh, dm, jnp.zeros_like(dh)], axis=1)], axis=0)
    ex = _dot(lhs, e3_ref[...])
    acum_f = ex[:ch]
    dt_f = ex[ch:]
    xdt = xs * dt_f
    last = acum_f[ch - 1:ch, :]
    xdt_b = xdt.astype(BF16)
    xd_b = (xdt * jnp.exp(last - acum_f)).astype(BF16)
    eacum = jnp.exp(acum_f)
    cdecay = jnp.exp(last)

    lane = lax.broadcasted_iota(jnp.int32, (ch, LANES), 1)
    ys = []
    for g in range(SSM_GROUPS):
        bg = bm[:, g * SSM_STATE:(g + 1) * SSM_STATE]
        cg = cm[:, g * SSM_STATE:(g + 1) * SSM_STATE]
        cb = _dot_nt(cg, bg)
        parts = []
        for pp in range(_GROUP_W // LANES):
            ms = []
            for hh in range(2):
                h = g * (SSM_HEADS // SSM_GROUPS) + 2 * pp + hh
                seg = acum[:, h:h + 1] - acum_t[h:h + 1, :]
                ms.append((cb * jnp.where(causal, jnp.exp(seg), 0.0)).astype(BF16))
            c0 = g * _GROUP_W + pp * LANES
            xp = xdt_b[:, c0:c0 + LANES]
            x2 = jnp.concatenate([jnp.where(lane < SSM_HEADDIM, xp, 0.0).astype(BF16),
                                  jnp.where(lane >= SSM_HEADDIM, xp, 0.0).astype(BF16)], axis=0)
            parts.append(_dot(jnp.concatenate(ms, axis=1), x2))
        gs = slice(g * _GROUP_W, (g + 1) * _GROUP_W)
        st = st_ref[g]
        y_off = _dot(cg, st.astype(BF16)) * eacum[:, gs]
        st_ref[g] = st * cdecay[:, gs] + _dot_tn(bg, xd_b[:, gs])
        ys.append(jnp.concatenate(parts, axis=1) + y_off)
    y = jnp.concatenate(ys, axis=1) + xs * dfull_ref[...]

    gz = y * _silu(z_ref[...].astype(F32))
    outs = []
    for g in range(SSM_GROUPS):
        gg = gz[:, g * _GROUP_W:(g + 1) * _GROUP_W]
        outs.append(gg * lax.rsqrt(jnp.mean(gg * gg, axis=-1, keepdims=True) + EPS))
    o_ref[...] = (jnp.concatenate(outs, axis=1) * nw_ref[...]).astype(BF16)


def _ssd(xbc, z, dt, cw, cb, dtb, alog, dfull, nw, e3, bsz, seq):
    t = xbc.shape[0]
    ch = SSM_CHUNK
    per_b = seq // ch
    tile = lambda w: pl.BlockSpec((ch, w), lambda b, s: (b * per_b + s, 0))
    return pl.pallas_call(
        _ssd_kernel,
        out_shape=jax.ShapeDtypeStruct((t, SSM_D_INNER), BF16),
        grid=(bsz, per_b),
        in_specs=[tile(SSM_XBC), tile(SSM_D_INNER), tile(LANES)]
        + [_resident(a.shape) for a in (cw, cb, dtb, alog, dfull, nw, e3)],
        out_specs=tile(SSM_D_INNER),
        scratch_shapes=[pltpu.VMEM((ch + _SSM_HALO, SSM_XBC), F32),
                        pltpu.VMEM((SSM_GROUPS, SSM_STATE, _GROUP_W), F32)],
        compiler_params=_cparams(("parallel", "arbitrary")),
        name="ssd",
    )(xbc, z, dt, cw, cb, dtb, alog, dfull, nw, e3)


_VA_ROWS = 80


def _moba_prep_kernel(q_ref, k_ref, v_ref, qa_ref, ka_ref, va_ref, km_ref, *, seq):
    nb = seq // MOBA_BLOCK
    hd = ATTN_HEAD_DIM
    q = q_ref[...]
    k = k_ref[...].astype(F32)
    for j in range(nb):
        km_ref[j:j + 1, :] = jnp.mean(k[j * MOBA_BLOCK:(j + 1) * MOBA_BLOCK, :], axis=0, keepdims=True)
    km = km_ref[...]
    qt = q.astype(F32).T * (hd ** -0.5)
    vt = v_ref[...].astype(F32).T

    shift = MOBA_BLOCK.bit_length() - 1
    lane_k = lax.broadcasted_iota(jnp.int32, (seq, LANES), 1)
    key_blk = lax.shift_right_logical(lax.broadcasted_iota(jnp.int32, (seq, LANES), 0), shift)
    lane_m = lax.broadcasted_iota(jnp.int32, (nb, LANES), 1)
    sub = lax.broadcasted_iota(jnp.int32, (nb, seq), 0).astype(F32)
    qblk = lax.shift_right_logical(lax.broadcasted_iota(jnp.int32, (1, seq), 1), shift).astype(F32)
    pad_rows = jnp.zeros((LANES - hd - nb, seq), BF16)
    ones_row = jnp.where(lax.broadcasted_iota(jnp.int32, (_VA_ROWS - hd, seq), 0) == 0, 1.0, 0.0).astype(BF16)

    for hh in range(2):
        lo = hh * hd
        in_head = (lane_m >= lo) & (lane_m < lo + hd)
        kmh = jnp.where(in_head, km, 0.0)
        g3 = _dot_nt(jnp.concatenate(_split3(kmh), axis=0), q)
        gate = g3[:nb] + g3[nb:2 * nb] + g3[2 * nb:]
        gate = jnp.where(sub < qblk, gate, -jnp.inf)
        sel = sub == qblk
        for r in range(MOBA_TOPK):
            mx = jnp.max(gate, axis=0, keepdims=True)
            idx = jnp.min(jnp.where(gate == mx, sub, float(nb)), axis=0, keepdims=True)
            hit = sub == idx
            sel = sel | (hit & (qblk > r))
            gate = jnp.where(hit, -jnp.inf, gate)
        bias = jnp.where(sel, 0.0, NEG_BIG).astype(BF16)
        qh = qt[lo:lo + hd, :].astype(BF16)
        if hh == 0:
            qa_ref[0, hh] = jnp.concatenate([qh, bias, pad_rows], axis=0)
            onehot = (lane_k - hd) == key_blk
            ka = jnp.where(lane_k < hd, k, jnp.where(onehot, 1.0, 0.0))
        else:
            qa_ref[0, hh] = jnp.concatenate([bias, pad_rows, qh], axis=0)
            onehot = lane_k == key_blk
            ka = jnp.where(lane_k >= hd, k, jnp.where(onehot, 1.0, 0.0))
        ka_ref[0, 0, :, hh * LANES:(hh + 1) * LANES] = ka.astype(BF16)
        va_ref[0, 0, hh * _VA_ROWS:(hh + 1) * _VA_ROWS, :] = jnp.concatenate(
            [vt[lo:lo + hd, :].astype(BF16), ones_row], axis=0)


def _moba_prep(qkv, bsz, seq):
    pairs = ATTN_HEADS // 2
    nb = seq // MOBA_BLOCK
    assert 2 * nb <= LANES - ATTN_HEAD_DIM
    col = lambda off: pl.BlockSpec((seq, LANES), lambda b, p: (b, off + p))
    return pl.pallas_call(
        functools.partial(_moba_prep_kernel, seq=seq),
        out_shape=[jax.ShapeDtypeStruct((bsz, ATTN_HEADS, LANES, seq), BF16),
                   jax.ShapeDtypeStruct((bsz, pairs, seq, 2 * LANES), BF16),
                   jax.ShapeDtypeStruct((bsz, pairs, 2 * _VA_ROWS, seq), BF16)],
        grid=(bsz, pairs),
        in_specs=[col(0), col(pairs), col(2 * pairs)],
        out_specs=[pl.BlockSpec((1, 2, LANES, seq), lambda b, p: (b, p, 0, 0)),
                   pl.BlockSpec((1, 1, seq, 2 * LANES), lambda b, p: (b, p, 0, 0)),
                   pl.BlockSpec((1, 1, 2 * _VA_ROWS, seq), lambda b, p: (b, p, 0, 0))],
        scratch_shapes=[pltpu.VMEM((nb, LANES), F32)],
        compiler_params=_cparams(("parallel", "parallel")),
        name="moba_prep",
    )(qkv, qkv, qkv)


_MOBA_SWEEP = 4


def _moba_attn_kernel(qa_ref, ka_ref, va_ref, o_ref):
    i = pl.program_id(2)
    tq = MOBA_BLOCK
    hd = ATTN_HEAD_DIM
    ck = _MOBA_SWEEP * MOBA_BLOCK
    own = lax.shift_right_logical(i, _MOBA_SWEEP.bit_length() - 1)

    zq = jnp.zeros((LANES, tq), BF16)
    qbd = jnp.concatenate([jnp.concatenate([qa_ref[0, 0], zq], axis=1),
                           jnp.concatenate([zq, qa_ref[0, 1]], axis=1)], axis=0)

    def scores(c):
        start = pl.multiple_of(c * ck, ck)
        return _dot(ka_ref[0, 0, pl.ds(start, ck), :], qbd)

    def update(c, st, m, acc):
        start = pl.multiple_of(c * ck, ck)
        vc = va_ref[0, 0, :, pl.ds(start, ck)]
        m_new = jnp.maximum(m, jnp.max(st, axis=0, keepdims=True))
        p = jnp.exp(st - m_new)
        return m_new, acc * jnp.exp(m - m_new) + _dot(vc, p.astype(BF16))

    kpos = own * ck + lax.broadcasted_iota(jnp.int32, (ck, 2 * tq), 0)
    qpos = i * tq + (lax.broadcasted_iota(jnp.int32, (ck, 2 * tq), 1) & (tq - 1))
    s_own = jnp.where(kpos <= qpos, scores(own), NEG_BIG)

    def body(c, carry):
        cur, st, m, acc = carry
        st_next = scores(c)
        m, acc = update(cur, st, m, acc)
        return c, st_next, m, acc

    init = (own, s_own, jnp.full((1, 2 * tq), -jnp.inf, F32), jnp.zeros((2 * _VA_ROWS, 2 * tq), F32))
    cur, st, m, acc = lax.fori_loop(0, own, body, init)
    m, acc = update(cur, st, m, acc)
    ot = jnp.concatenate([acc[:hd, :tq] / acc[hd:hd + 1, :tq],
                          acc[_VA_ROWS:_VA_ROWS + hd, tq:] / acc[_VA_ROWS + hd:_VA_ROWS + hd + 1, tq:]], axis=0)
    o_ref[...] = ot.T.astype(BF16)


def _moba_attn(qa, ka, va, bsz, seq):
    pairs = ATTN_HEADS // 2
    nq = seq // MOBA_BLOCK
    ck = _MOBA_SWEEP * MOBA_BLOCK
    assert seq % ck == 0
    return pl.pallas_call(
        _moba_attn_kernel,
        out_shape=jax.ShapeDtypeStruct((bsz * seq, ATTN_WIDTH), BF16),
        grid=(bsz, pairs, nq),
        in_specs=[pl.BlockSpec((1, 2, LANES, MOBA_BLOCK), lambda b, p, i: (b, p, 0, i)),
                  pl.BlockSpec((1, 1, seq, 2 * LANES), lambda b, p, i: (b, p, 0, 0)),
                  pl.BlockSpec((1, 1, 2 * _VA_ROWS, seq), lambda b, p, i: (b, p, 0, 0))],
        out_specs=pl.BlockSpec((MOBA_BLOCK, LANES), lambda b, p, i: (b * nq + i, p)),
        compiler_params=_cparams(("parallel", "parallel", "arbitrary")),
        name="moba_attn",
    )(qa, ka, va)


def _mix_kernel(x_ref, u_ref, y_ref, a_ref, g_ref, mod_ref, wc_ref, ws_ref, wa_ref, wm_ref, o_ref):
    d = D_MODEL
    out_a = _dot(u_ref[...], wc_ref[...])
    out_b = _dot(y_ref[...], ws_ref[...])
    out_c = _dot(a_ref[...], wa_ref[...])
    gt = jax.nn.sigmoid(g_ref[...].astype(F32))
    merged = gt[:, :d] * out_a + gt[:, d:2 * d] * out_b + gt[:, 2 * d:] * out_c
    o_ref[...] = x_ref[...] + mod_ref[0, 2:3, :] * _dot(merged.astype(BF16), wm_ref[...])


def _mix(x2, u, y, a, gate_in, mod_l, wc, ws, wa, wm, seq, tm=512):
    t, d = x2.shape
    per_b = seq // tm
    tile = lambda w: pl.BlockSpec((tm, w), lambda i: (i, 0))
    return pl.pallas_call(
        _mix_kernel,
        out_shape=jax.ShapeDtypeStruct((t, d), F32),
        grid=(t // tm,),
        in_specs=[tile(d), tile(CONF_WIDTH), tile(SSM_D_INNER), tile(ATTN_WIDTH), tile(N_BRANCHES * d),
                  pl.BlockSpec((1, 6, d), lambda i: (i // per_b, 0, 0)),
                  _resident(wc.shape), _resident(ws.shape), _resident(wa.shape), _resident(wm.shape)],
        out_specs=tile(d),
        compiler_params=_cparams(("parallel",)),
        name="mix_out",
    )(x2, u, y, a, gate_in, mod_l, wc, ws, wa, wm)


def _residual_out(x, gate, f, fw, final_norm):
    y = x + gate * f
    if final_norm:
        y = y * lax.rsqrt(jnp.mean(y * y, axis=-1, keepdims=True) + EPS) * fw
    return y


def _ffn_kernel(x_ref, nw_ref, mod_ref, wg_ref, wu_ref, wd_ref, fw_ref, o_ref, hb_ref, acc_ref, *, final_norm):
    f = pl.program_id(1)

    @pl.when(f == 0)
    def _():
        h = _modulated_norm(x_ref[...], nw_ref[...], mod_ref[0, 3:4, :], mod_ref[0, 4:5, :])
        hb_ref[...] = h.astype(BF16)
        acc_ref[...] = jnp.zeros(acc_ref.shape, F32)

    hb = hb_ref[...]
    act = (_silu(_dot(hb, wg_ref[...])) * _dot(hb, wu_ref[...])).astype(BF16)
    acc_ref[...] += _dot(act, wd_ref[...])

    @pl.when(f == pl.num_programs(1) - 1)
    def _():
        o_ref[...] = _residual_out(x_ref[...], mod_ref[0, 5:6, :], acc_ref[...], fw_ref[...], final_norm)


def _ffn(x2, nw, mod_l, wg, wu, wd, fw, seq, *, tm, tf, final_norm):
    t, d = x2.shape
    ff = wg.shape[1]
    per_b = seq // tm
    return pl.pallas_call(
        functools.partial(_ffn_kernel, final_norm=final_norm),
        out_shape=jax.ShapeDtypeStruct((t, d), F32),
        grid=(t // tm, ff // tf),
        in_specs=[pl.BlockSpec((tm, d), lambda i, f: (i, 0)),
                  _resident((1, d)),
                  pl.BlockSpec((1, 6, d), lambda i, f: (i // per_b, 0, 0)),
                  pl.BlockSpec((d, tf), lambda i, f: (0, f)),
                  pl.BlockSpec((d, tf), lambda i, f: (0, f)),
                  pl.BlockSpec((tf, d), lambda i, f: (f, 0)),
                  _resident((1, d))],
        out_specs=pl.BlockSpec((tm, d), lambda i, f: (i, 0)),
        scratch_shapes=[pltpu.VMEM((tm, d), BF16), pltpu.VMEM((tm, d), F32)],
        compiler_params=_cparams(("parallel", "arbitrary")),
        name="ffn_dense",
    )(x2, nw, mod_l, wg, wu, wd, fw)


_MOE_TM = 1024
_PAIRS = D_MODEL // 2
_ROW_WORDS = _PAIRS + LANES


def _pack_pairs(v):
    lo = lax.bitcast_convert_type(v[:, :_PAIRS].astype(BF16).astype(F32), jnp.uint32)
    hi = lax.bitcast_convert_type(v[:, _PAIRS:].astype(BF16).astype(F32), jnp.uint32)
    return hi | lax.shift_right_logical(lo, jnp.uint32(16))


def _unpack_pairs(u):
    lo = lax.bitcast_convert_type(lax.shift_left(u, jnp.uint32(16)), F32)
    hi = lax.bitcast_convert_type(u & jnp.uint32(0xFFFF0000), F32)
    return jnp.concatenate([lo, hi], axis=1)


def _row_copy(src_ref, src_row, dst_ref, dst_row, sem):
    return pltpu.make_async_copy(src_ref.at[pl.ds(src_row, 1), :], dst_ref.at[pl.ds(dst_row, 1), :], sem)


def _moe_route_kernel(x_ref, nw_ref, mod_ref, rwt_ref, rb_ref, xs_ref, dest_ref, cnt_ref,
                      rows_ref, dvm_ref, dsm_ref, carry_ref, row_sem, idx_sem, *, tm, cap, n_experts):
    step = pl.program_id(0)

    @pl.when(step == 0)
    def _():
        carry_ref[...] = jnp.zeros(carry_ref.shape, F32)

    h = _modulated_norm(x_ref[...], nw_ref[...], mod_ref[0, 3:4, :], mod_ref[0, 4:5, :])
    hb = h.astype(BF16)
    l3 = _dot_nt(rwt_ref[...], hb)
    logits = (l3[:LANES] + l3[LANES:2 * LANES] + l3[2 * LANES:] + rb_ref[...])[:n_experts]
    sub = lax.broadcasted_iota(jnp.int32, logits.shape, 0).astype(F32)
    m1 = jnp.max(logits, axis=0, keepdims=True)
    i1 = jnp.min(jnp.where(logits == m1, sub, float(n_experts)), axis=0, keepdims=True)
    rest = jnp.where(sub == i1, -jnp.inf, logits)
    m2 = jnp.max(rest, axis=0, keepdims=True)
    i2 = jnp.min(jnp.where(rest == m2, sub, float(n_experts)), axis=0, keepdims=True)
    e2 = jnp.exp(m2 - m1)
    w1 = 1.0 / (1.0 + e2)
    w2 = e2 / (1.0 + e2)

    hit1 = sub == i1
    hit2 = sub == i2
    picks = jnp.where(hit1 | hit2, 1.0, 0.0)
    picks16 = jnp.concatenate([picks, jnp.zeros((16 - n_experts, tm), F32)], axis=0).astype(BF16)
    before = lax.broadcasted_iota(jnp.int32, (tm, tm), 0) < lax.broadcasted_iota(jnp.int32, (tm, tm), 1)
    prefix = _dot(picks16, jnp.where(before, 1.0, 0.0).astype(BF16))[:n_experts]
    carry = carry_ref[0:n_experts, 0:1]
    rank = prefix + carry
    rank1 = jnp.sum(jnp.where(hit1, rank, 0.0), axis=0, keepdims=True)
    rank2 = jnp.sum(jnp.where(hit2, rank, 0.0), axis=0, keepdims=True)
    total = carry + jnp.sum(picks, axis=1, keepdims=True)
    carry_ref[0:n_experts, :] = jnp.broadcast_to(total, (n_experts, LANES))
    cnt_ref[...] = jnp.broadcast_to(total, (n_experts, LANES))

    dest = jnp.concatenate([i1 * float(cap) + rank1, i2 * float(cap) + rank2,
                            jnp.zeros((6, tm), F32)], axis=0).astype(jnp.int32)
    dest_ref[...] = dest
    dvm_ref[...] = dest
    idx_copy = pltpu.make_async_copy(dvm_ref, dsm_ref, idx_sem)
    idx_copy.start()

    packed = _pack_pairs(h)
    zeros = jnp.zeros((LANES - 2, tm), F32)
    for k, (wa, wb) in enumerate(((w1, w2), (w2, w1))):
        wt = jnp.concatenate([wa, wb, zeros], axis=0).T
        rows_ref[k, :, 0:_PAIRS] = packed
        rows_ref[k, :, _PAIRS:] = lax.bitcast_convert_type(wt, jnp.uint32)
    idx_copy.wait()

    def issue(t, _):
        _row_copy(rows_ref.at[0], t, xs_ref, dsm_ref[0, t], row_sem).start()
        _row_copy(rows_ref.at[1], t, xs_ref, dsm_ref[1, t], row_sem).start()
        return 0

    lax.fori_loop(0, tm, issue, 0, unroll=8)

    def drain(t, _):
        _row_copy(rows_ref.at[0], 0, xs_ref, 0, row_sem).wait()
        _row_copy(rows_ref.at[1], 0, xs_ref, 0, row_sem).wait()
        return 0

    lax.fori_loop(0, tm, drain, 0, unroll=8)


def _moe_route(x2, nw, mod_l, rwt3, rb_col, seq, cap, n_experts, tm=512):
    t, d = x2.shape
    per_b = seq // tm
    return pl.pallas_call(
        functools.partial(_moe_route_kernel, tm=tm, cap=cap, n_experts=n_experts),
        out_shape=[jax.ShapeDtypeStruct((n_experts * cap, _ROW_WORDS), jnp.uint32),
                   jax.ShapeDtypeStruct((8, t), jnp.int32),
                   jax.ShapeDtypeStruct((n_experts, LANES), F32)],
        grid=(t // tm,),
        in_specs=[pl.BlockSpec((tm, d), lambda i: (i, 0)),
                  _resident((1, d)),
                  pl.BlockSpec((1, 6, d), lambda i: (i // per_b, 0, 0)),
                  _resident(rwt3.shape), _resident(rb_col.shape)],
        out_specs=[pl.BlockSpec(memory_space=pl.ANY),
                   pl.BlockSpec((8, tm), lambda i: (0, i)),
                   pl.BlockSpec((n_experts, LANES), lambda i: (0, 0))],
        scratch_shapes=[pltpu.VMEM((2, tm, _ROW_WORDS), jnp.uint32),
                        pltpu.VMEM((8, tm), jnp.int32),
                        pltpu.SMEM((8, tm), jnp.int32),
                        pltpu.VMEM((8, LANES), F32),
                        pltpu.SemaphoreType.DMA, pltpu.SemaphoreType.DMA],
        compiler_params=_cparams(("arbitrary",)),
        name="moe_route",
    )(x2, nw, mod_l, rwt3, rb_col)


def _moe_tails_kernel(cnt_ref, xs_in_ref, xs_ref, zero_ref, sem, *, cap, n_experts):
    del xs_in_ref
    zero_ref[...] = jnp.zeros(zero_ref.shape, jnp.uint32)
    for e in range(n_experts):
        lo = cnt_ref[e]
        hi = (lo + (_MOE_TM - 1)) & (-_MOE_TM)

        def issue(r, _):
            _row_copy(zero_ref, 0, xs_ref, e * cap + r, sem).start()
            return 0

        def drain(r, _):
            _row_copy(zero_ref, 0, xs_ref, 0, sem).wait()
            return 0

        lax.fori_loop(lo, hi, issue, 0)
        lax.fori_loop(lo, hi, drain, 0)


def _moe_tails(counts, xs, cap, n_experts):
    return pl.pallas_call(
        functools.partial(_moe_tails_kernel, cap=cap, n_experts=n_experts),
        out_shape=jax.ShapeDtypeStruct(xs.shape, xs.dtype),
        grid_spec=pltpu.PrefetchScalarGridSpec(
            num_scalar_prefetch=1, grid=(1,),
            in_specs=[pl.BlockSpec(memory_space=pl.ANY)],
            out_specs=pl.BlockSpec(memory_space=pl.ANY),
            scratch_shapes=[pltpu.VMEM((8, _ROW_WORDS), jnp.uint32), pltpu.SemaphoreType.DMA]),
        input_output_aliases={1: 0},
        compiler_params=_cparams(("arbitrary",)),
        name="moe_tails",
    )(counts, xs)


def _moe_expert_kernel(blk_ref, exp_ref, used_ref, x_ref, wg_ref, wu_ref, wd_ref, o_ref, hb_ref, acc_ref):
    del blk_ref, exp_ref
    g = pl.program_id(0)
    f = pl.program_id(1)

    @pl.when(g < used_ref[0])
    def _():
        @pl.when(f == 0)
        def _():
            hb_ref[...] = _unpack_pairs(x_ref[:, 0:_PAIRS]).astype(BF16)
            acc_ref[...] = jnp.zeros(acc_ref.shape, F32)

        hb = hb_ref[...]
        act = (_silu(_dot(hb, wg_ref[0])) * _dot(hb, wu_ref[0])).astype(BF16)
        acc_ref[...] += _dot(act, wd_ref[0])

        @pl.when(f == pl.num_programs(1) - 1)
        def _():
            wcol = lax.bitcast_convert_type(x_ref[:, _PAIRS:_PAIRS + 1], F32)
            o_ref[...] = _pack_pairs(acc_ref[...] * wcol)


def _moe_experts(blk, exp, used, xs, wg, wu, wd, n_tiles, tf=896):
    n_experts, d, ff = wg.shape
    tm = _MOE_TM
    return pl.pallas_call(
        _moe_expert_kernel,
        out_shape=jax.ShapeDtypeStruct((xs.shape[0], _PAIRS), jnp.uint32),
        grid_spec=pltpu.PrefetchScalarGridSpec(
            num_scalar_prefetch=3, grid=(n_tiles, ff // tf),
            in_specs=[pl.BlockSpec((tm, _ROW_WORDS), lambda g, f, blk, exp, used: (blk[g], 0)),
                      pl.BlockSpec((1, d, tf), lambda g, f, blk, exp, used: (exp[g], 0, f)),
                      pl.BlockSpec((1, d, tf), lambda g, f, blk, exp, used: (exp[g], 0, f)),
                      pl.BlockSpec((1, tf, d), lambda g, f, blk, exp, used: (exp[g], f, 0))],
            out_specs=pl.BlockSpec((tm, _PAIRS), lambda g, f, blk, exp, used: (blk[g], 0)),
            scratch_shapes=[pltpu.VMEM((tm, d), BF16), pltpu.VMEM((tm, d), F32)]),
        compiler_params=_cparams(("arbitrary", "arbitrary")),
        name="moe_experts",
    )(blk, exp, used, xs, wg, wu, wd)


def _moe_combine_kernel(dest_ref, x_ref, mod_ref, fw_ref, ys_ref, o_ref, buf_ref, sems, *, tm, n_tok, final_norm):
    step = pl.program_id(0)
    n_steps = pl.num_programs(0)

    def start_gather(s, slot):
        def body(t, _):
            for k in range(2):
                _row_copy(ys_ref, dest_ref[k * n_tok + s * tm + t], buf_ref.at[slot, k], t, sems.at[slot]).start()
            return 0
        lax.fori_loop(0, tm, body, 0, unroll=8)

    def wait_gather(slot):
        def body(t, _):
            for k in range(2):
                _row_copy(ys_ref, 0, buf_ref.at[slot, k], 0, sems.at[slot]).wait()
            return 0
        lax.fori_loop(0, tm, body, 0, unroll=8)

    @pl.when(step == 0)
    def _():
        start_gather(0, 0)

    slot = step & 1

    @pl.when(step + 1 < n_steps)
    def _():
        start_gather(step + 1, 1 - slot)

    wait_gather(slot)
    f = _unpack_pairs(buf_ref[slot, 0]) + _unpack_pairs(buf_ref[slot, 1])
    o_ref[...] = _residual_out(x_ref[...], mod_ref[0, 5:6, :], f, fw_ref[...], final_norm)


def _moe_combine(dest_flat, x2, mod_l, fw, ys, seq, *, final_norm, tm=256):
    t, d = x2.shape
    per_b = seq // tm
    return pl.pallas_call(
        functools.partial(_moe_combine_kernel, tm=tm, n_tok=t, final_norm=final_norm),
        out_shape=jax.ShapeDtypeStruct((t, d), F32),
        grid_spec=pltpu.PrefetchScalarGridSpec(
            num_scalar_prefetch=1, grid=(t // tm,),
            in_specs=[pl.BlockSpec((tm, d), lambda i, dest: (i, 0)),
                      pl.BlockSpec((1, 6, d), lambda i, dest: (i // per_b, 0, 0)),
                      pl.BlockSpec((1, d), lambda i, dest: (0, 0)),
                      pl.BlockSpec(memory_space=pl.ANY)],
            out_specs=pl.BlockSpec((tm, d), lambda i, dest: (i, 0)),
            scratch_shapes=[pltpu.VMEM((2, 2, tm, _PAIRS), jnp.uint32), pltpu.SemaphoreType.DMA((2,))]),
        compiler_params=_cparams(("arbitrary",)),
        name="moe_combine",
    )(dest_flat, x2, mod_l, fw, ys)


def _moe(x2, nw, mod_l, router_w, router_b, wg, wu, wd, fw, seq, *, final_norm):
    t, d = x2.shape
    n_experts = wg.shape[0]
    tm = _MOE_TM
    cap = (t // tm + 1) * tm
    rwt = jnp.zeros((LANES, d), F32).at[:n_experts].set(router_w.T)
    rwt3 = jnp.concatenate(_split3(rwt), axis=0)
    rb_col = jnp.zeros((LANES, 1), F32).at[:n_experts, 0].set(router_b)
    xs, dest, cnt = _moe_route(x2, nw, mod_l, rwt3, rb_col, seq, cap, n_experts)
    counts = cnt[:, 0].astype(jnp.int32)
    xs = _moe_tails(counts, xs, cap, n_experts)

    n_tiles = 2 * t // tm + n_experts
    tiles = (counts + tm - 1) // tm
    first = jnp.cumsum(tiles) - tiles
    used = jnp.sum(tiles)
    g = jnp.minimum(jnp.arange(n_tiles, dtype=jnp.int32), used - 1)
    exp = jnp.sum((g[:, None] >= first[None, :]).astype(jnp.int32), axis=1) - 1
    blk = exp * (cap // tm) + g - first[exp]
    ys = _moe_experts(blk.astype(jnp.int32), exp.astype(jnp.int32), used.reshape(1).astype(jnp.int32),
                      xs, wg, wu, wd, n_tiles)
    dest_flat = jnp.concatenate([dest[0], dest[1]])
    return _moe_combine(dest_flat, x2, mod_l, fw, ys, seq, final_norm=final_norm)


def _pad_lanes(v):
    return jnp.zeros((1, LANES), F32).at[0, :v.shape[0]].set(v)


def kernel(x, c, norm1_w, norm2_w, ada_w, ada_b, w_in, conf_conv_w, conf_conv_b, conf_ln_w, conf_ln_b,
           conf_out, ssm_conv_w, ssm_conv_b, ssm_dt_bias, ssm_A_log, ssm_D, ssm_norm_w, ssm_out, attn_out,
           mix_out, ffn_w_gate, ffn_w_up, ffn_w_down, moe_router_w, moe_router_b, moe_w_gate, moe_w_up,
           moe_w_down, final_norm_w):
    bsz, seq, d = x.shape
    depth = w_in.shape[0]
    t = bsz * seq
    row = lambda v: v.reshape(1, -1).astype(F32)

    mod = _ada_mod(c, ada_w, ada_b).reshape(depth, bsz, 6, d)
    head_of = jnp.arange(SSM_D_INNER) // SSM_HEADDIM
    e1 = (jnp.arange(LANES)[:, None] == head_of[None, :]).astype(BF16)
    e3 = jnp.concatenate([e1, e1, e1], axis=0)

    x2 = x.reshape(t, d)
    edges = [0]
    for wdt in (2 * CONF_WIDTH, SSM_D_INNER, SSM_XBC, SSM_HEADS, 3 * ATTN_WIDTH, N_BRANCHES * D_MODEL):
        edges.append(edges[-1] + wdt)
    for l in range(depth):
        wl = w_in[l]
        parts = [wl[:, edges[i]:edges[i + 1]] for i in range(6)]
        parts[3] = jnp.zeros((d, LANES), F32).at[:, :SSM_HEADS].set(parts[3])
        ws = [p.astype(BF16) for p in parts]
        conf_in, z, xbc, dt, qkv, gate_in = _inproj(x2, row(norm1_w[l]), mod[l], ws, seq)

        u = _conformer(conf_in, conf_conv_w[l], row(conf_conv_b[l]), row(conf_ln_w[l]), row(conf_ln_b[l]),
                       bsz, seq)
        y = _ssd(xbc, z, dt, ssm_conv_w[l], row(ssm_conv_b[l]), _pad_lanes(ssm_dt_bias[l]),
                 _pad_lanes(ssm_A_log[l]), row(jnp.repeat(ssm_D[l], SSM_HEADDIM)), row(ssm_norm_w[l]), e3,
                 bsz, seq)
        qa, ka, va = _moba_prep(qkv, bsz, seq)
        a = _moba_attn(qa, ka, va, bsz, seq)
        x2 = _mix(x2, u, y, a, gate_in, mod[l], conf_out[l].astype(BF16), ssm_out[l].astype(BF16),
                  attn_out[l].astype(BF16), mix_out[l].astype(BF16), seq)

        final = l == depth - 1
        j = l // 2
        if l % 2 == 0:
            x2 = _ffn(x2, row(norm2_w[l]), mod[l], ffn_w_gate[j].astype(BF16), ffn_w_up[j].astype(BF16),
                      ffn_w_down[j].astype(BF16), row(final_norm_w), seq, tm=512, tf=1408, final_norm=final)
        else:
            x2 = _moe(x2, row(norm2_w[l]), mod[l], moe_router_w[j], moe_router_b[j],
                      moe_w_gate[j].astype(BF16), moe_w_up[j].astype(BF16), moe_w_down[j].astype(BF16),
                      row(final_norm_w), seq, final_norm=final)
    return x2.reshape(bsz, seq, d)
```

```python
import functools

import jax
import jax.numpy as jnp
from jax import lax
from jax.experimental import pallas as pl
from jax.experimental.pallas import tpu as pltpu

F32 = jnp.float32
BF16 = jnp.bfloat16

D_MODEL = 1024
CONF_WIDTH = 512
CONF_KERNEL = 31
SSM_D_INNER = 1024
SSM_HEADDIM = 64
SSM_HEADS = 16
SSM_GROUPS = 2
SSM_STATE = 128
SSM_CONV = 4
SSM_CHUNK = 128
SSM_XBC = SSM_D_INNER + 2 * SSM_GROUPS * SSM_STATE
ATTN_HEADS = 8
ATTN_HEAD_DIM = 64
ATTN_WIDTH = ATTN_HEADS * ATTN_HEAD_DIM
MOBA_BLOCK = 256
MOBA_TOPK = 3
N_EXPERTS = 8
N_BRANCHES = 3
EPS = 1e-6

LANES = 128
VMEM_LIMIT = 56 * 1024 * 1024
NEG_BIG = -1e30


def _cparams(sem, vmem=VMEM_LIMIT):
    return pltpu.CompilerParams(dimension_semantics=sem, vmem_limit_bytes=vmem)


def _resident(shape):
    zeros = (0,) * len(shape)
    return pl.BlockSpec(shape, lambda *_: zeros, pipeline_mode=pl.Buffered(1))


def _silu(v):
    return v * jax.nn.sigmoid(v)


def _split2(v):
    hi = v.astype(BF16)
    mid = (v - hi.astype(F32)).astype(BF16)
    return hi, mid


def _split3(v):
    hi = v.astype(BF16)
    r = v - hi.astype(F32)
    mid = r.astype(BF16)
    lo = (r - mid.astype(F32)).astype(BF16)
    return hi, mid, lo


def _dot(a, b):
    return jnp.dot(a, b, preferred_element_type=F32)


def _dot_nt(a, b):
    return lax.dot_general(a, b, (((1,), (1,)), ((), ())), preferred_element_type=F32)


def _dot_tn(a, b):
    return lax.dot_general(a, b, (((0,), (0,)), ((), ())), preferred_element_type=F32)


def _ada_kernel(c_ref, w_ref, b_ref, o_ref):
    c = c_ref[...]
    ch, cm = _split2(_silu(c))
    wh, wm = _split2(w_ref[0])
    o_ref[0] = _dot(ch, wh) + _dot(ch, wm) + _dot(cm, wh) + b_ref[0]


def _ada_mod(c, ada_w, ada_b):
    n_layers, d, n = ada_w.shape
    bsz = c.shape[0]
    rows = 8
    tn = 1536
    cpad = jnp.zeros((rows, d), F32).at[:bsz].set(c)
    out = pl.pallas_call(
        _ada_kernel,
        out_shape=jax.ShapeDtypeStruct((n_layers, rows, n), F32),
        grid=(n_layers, n // tn),
        in_specs=[
            pl.BlockSpec((rows, d), lambda l, j: (0, 0)),
            pl.BlockSpec((1, d, tn), lambda l, j: (l, 0, j)),
            pl.BlockSpec((1, 1, tn), lambda l, j: (l, 0, j)),
        ],
        out_specs=pl.BlockSpec((1, rows, tn), lambda l, j: (l, 0, j)),
        compiler_params=_cparams(("parallel", "parallel")),
        name="ada_mod",
    )(cpad, ada_w, ada_b.reshape(n_layers, 1, n))
    return out[:, :bsz]


def _modulated_norm(x, nw, shift, scale):
    ms = jnp.mean(x * x, axis=-1, keepdims=True)
    return (x * lax.rsqrt(ms + EPS) * nw) * (1.0 + scale) + shift


def _inproj_kernel(x_ref, nw_ref, mod_ref, wc_ref, wz_ref, wx_ref, wd_ref, wq_ref, wg_ref,
                   oc_ref, oz_ref, ox_ref, od_ref, oq_ref, og_ref):
    h = _modulated_norm(x_ref[...], nw_ref[...], mod_ref[0, 0:1, :], mod_ref[0, 1:2, :])
    hb = h.astype(BF16)
    oc_ref[...] = _dot(hb, wc_ref[...]).astype(BF16)
    oz_ref[...] = _dot(hb, wz_ref[...]).astype(BF16)
    ox_ref[...] = _dot(hb, wx_ref[...]).astype(BF16)
    od_ref[...] = _dot(hb, wd_ref[...])
    oq_ref[...] = _dot(hb, wq_ref[...]).astype(BF16)
    og_ref[...] = _dot(hb, wg_ref[...]).astype(BF16)


def _inproj(x2, nw, mod_l, ws, seq, tm=512):
    t, d = x2.shape
    per_b = seq // tm
    widths = [w.shape[1] for w in ws]
    dts = [BF16, BF16, BF16, F32, BF16, BF16]
    return pl.pallas_call(
        _inproj_kernel,
        out_shape=[jax.ShapeDtypeStruct((t, wd), dt) for wd, dt in zip(widths, dts)],
        grid=(t // tm,),
        in_specs=[
            pl.BlockSpec((tm, d), lambda i: (i, 0)),
            _resident((1, d)),
            pl.BlockSpec((1, 6, d), lambda i: (i // per_b, 0, 0)),
        ] + [_resident(w.shape) for w in ws],
        out_specs=[pl.BlockSpec((tm, wd), lambda i: (i, 0)) for wd in widths],
        compiler_params=_cparams(("parallel",)),
        name="inproj",
    )(x2, nw, mod_l, *ws)


_CONF_HALO = 32
_CONF_ROWS = 32
_SUBLANES = 8


def _conf_kernel(cin_ref, w_ref, b_ref, lnw_ref, lnb_ref, o_ref, xg_ref, sh_ref, *, ts):
    s = pl.program_id(1)

    @pl.when(s == 0)
    def _():
        xg_ref[0:_CONF_HALO, :] = jnp.zeros((_CONF_HALO, CONF_WIDTH), F32)

    @pl.when(s > 0)
    def _():
        xg_ref[0:_CONF_HALO, :] = xg_ref[ts:ts + _CONF_HALO, :]

    cin = cin_ref[...].astype(F32)
    xg_ref[_CONF_HALO:_CONF_HALO + ts, :] = cin[:, :CONF_WIDTH] * jax.nn.sigmoid(cin[:, CONF_WIDTH:])
    rows = sh_ref.shape[1]
    for sft in range(1, _SUBLANES):
        sh_ref[sft - 1] = xg_ref[sft:sft + rows, :]
    first = _CONF_HALO - (CONF_KERNEL - 1)
    groups = _CONF_ROWS // _SUBLANES
    bias = jnp.broadcast_to(b_ref[...], (_SUBLANES, CONF_WIDTH))
    for r0 in range(0, ts, _CONF_ROWS):
        accs = [bias] * groups
        for k in range(CONF_KERNEL):
            sft = (first + k) % _SUBLANES
            base = r0 + first + k - sft
            wk = w_ref[k * _SUBLANES:(k + 1) * _SUBLANES, :]
            for g in range(groups):
                lo = base + g * _SUBLANES
                tap = sh_ref[sft - 1, lo:lo + _SUBLANES, :] if sft else xg_ref[lo:lo + _SUBLANES, :]
                accs[g] = accs[g] + wk * tap
        acc = jnp.concatenate(accs, axis=0)
        mu = jnp.mean(acc, axis=-1, keepdims=True)
        xc = acc - mu
        var = jnp.mean(xc * xc, axis=-1, keepdims=True)
        y = xc * lax.rsqrt(var + EPS) * lnw_ref[...] + lnb_ref[...]
        o_ref[r0:r0 + _CONF_ROWS, :] = _silu(y).astype(BF16)


def _conformer(conf_in, cw, cb, lnw, lnb, bsz, seq, ts=512):
    t = conf_in.shape[0]
    per_b = seq // ts
    cw = jnp.repeat(cw, _SUBLANES, axis=0)
    return pl.pallas_call(
        functools.partial(_conf_kernel, ts=ts),
        out_shape=jax.ShapeDtypeStruct((t, CONF_WIDTH), BF16),
        grid=(bsz, per_b),
        in_specs=[
            pl.BlockSpec((ts, 2 * CONF_WIDTH), lambda b, s: (b * per_b + s, 0)),
            _resident(cw.shape), _resident(cb.shape), _resident(lnw.shape), _resident(lnb.shape),
        ],
        out_specs=pl.BlockSpec((ts, CONF_WIDTH), lambda b, s: (b * per_b + s, 0)),
        scratch_shapes=[pltpu.VMEM((ts + _CONF_HALO, CONF_WIDTH), F32),
                        pltpu.VMEM((_SUBLANES - 1, ts + _CONF_HALO - _SUBLANES, CONF_WIDTH), F32)],
        compiler_params=_cparams(("parallel", "arbitrary")),
        name="conformer",
    )(conf_in, cw, cb, lnw, lnb)


_SSM_HALO = 8
_GROUP_W = SSM_D_INNER // SSM_GROUPS


def _ssd_kernel(xbc_ref, z_ref, dt_ref, cw_ref, cb_ref, dtb_ref, alog_ref, dfull_ref, nw_ref,
                e3_ref, o_ref, xh_ref, st_ref):
    ch = SSM_CHUNK
    s = pl.program_id(1)

    @pl.when(s == 0)
    def _():
        xh_ref[0:_SSM_HALO, :] = jnp.zeros((_SSM_HALO, SSM_XBC), F32)
        st_ref[...] = jnp.zeros(st_ref.shape, F32)

    @pl.when(s > 0)
    def _():
        xh_ref[0:_SSM_HALO, :] = xh_ref[ch:ch + _SSM_HALO, :]

    xh_ref[_SSM_HALO:_SSM_HALO + ch, :] = xbc_ref[...].astype(F32)
    first = _SSM_HALO - (SSM_CONV - 1)
    acc = jnp.broadcast_to(cb_ref[...], (ch, SSM_XBC))
    for k in range(SSM_CONV):
        acc = acc + cw_ref[k:k + 1, :] * xh_ref[first + k:first + k + ch, :]
    xc = _silu(acc)
    xs = xc[:, :SSM_D_INNER]
    bm = xc[:, SSM_D_INNER:SSM_D_INNER + SSM_GROUPS * SSM_STATE].astype(BF16)
    cm = xc[:, SSM_D_INNER + SSM_GROUPS * SSM_STATE:].astype(BF16)

    dtr = dt_ref[...] + dtb_ref[...]
    dt = jnp.maximum(dtr, 0.0) + jnp.log(1.0 + jnp.exp(-jnp.abs(dtr)))
    ad = dt * (-jnp.exp(alog_ref[...]))

    row = lax.broadcasted_iota(jnp.int32, (ch, ch), 0)
    col = lax.broadcasted_iota(jnp.int32, (ch, ch), 1)
    causal = row >= col
    tril = jnp.where(causal, 1.0, 0.0).astype(BF16)
    a3 = _dot(tril, jnp.concatenate(_split3(ad), axis=1))
    acum = a3[:, :LANES] + a3[:, LANES:2 * LANES] + a3[:, 2 * LANES:]
    acum_t = acum.T

    dh, dm = _split2(dt)
    lhs = jnp.concatenate([
        jnp.concatenate(_split3(acum), axis=1),
        jnp.concatenate([dh, dm, jnp.zeros_like(dh)], axis=1)], axis=0)
    ex = _dot(lhs, e3_ref[...])
    acum_f = ex[:ch]
    dt_f = ex[ch:]
    xdt = xs * dt_f
    last = acum_f[ch - 1:ch, :]
    xdt_b = xdt.astype(BF16)
    xd_b = (xdt * jnp.exp(last - acum_f)).astype(BF16)
    eacum = jnp.exp(acum_f)
    cdecay = jnp.exp(last)

    lane = lax.broadcasted_iota(jnp.int32, (ch, LANES), 1)
    ys = []
    for g in range(SSM_GROUPS):
        bg = bm[:, g * SSM_STATE:(g + 1) * SSM_STATE]
        cg = cm[:, g * SSM_STATE:(g + 1) * SSM_STATE]
        cb = _dot_nt(cg, bg)
        parts = []
        for pp in range(_GROUP_W // LANES):
            ms = []
            for hh in range(2):
                h = g * (SSM_HEADS // SSM_GROUPS) + 2 * pp + hh
                seg = acum[:, h:h + 1] - acum_t[h:h + 1, :]
                ms.append((cb * jnp.where(causal, jnp.exp(seg), 0.0)).astype(BF16))
            c0 = g * _GROUP_W + pp * LANES
            xp = xdt_b[:, c0:c0 + LANES]
            x2 = jnp.concatenate([jnp.where(lane < SSM_HEADDIM, xp, 0.0).astype(BF16),
                                  jnp.where(lane >= SSM_HEADDIM, xp, 0.0).astype(BF16)], axis=0)
            parts.append(_dot(jnp.concatenate(ms, axis=1), x2))
        gs = slice(g * _GROUP_W, (g + 1) * _GROUP_W)
        st = st_ref[g]
        y_off = _dot(cg, st.astype(BF16)) * eacum[:, gs]
        st_ref[g] = st * cdecay[:, gs] + _dot_tn(bg, xd_b[:, gs])
        ys.append(jnp.concatenate(parts, axis=1) + y_off)
    y = jnp.concatenate(ys, axis=1) + xs * dfull_ref[...]

    gz = y * _silu(z_ref[...].astype(F32))
    outs = []
    for g in range(SSM_GROUPS):
        gg = gz[:, g * _GROUP_W:(g + 1) * _GROUP_W]
        outs.append(gg * lax.rsqrt(jnp.mean(gg * gg, axis=-1, keepdims=True) + EPS))
    o_ref[...] = (jnp.concatenate(outs, axis=1) * nw_ref[...]).astype(BF16)


def _ssd(xbc, z, dt, cw, cb, dtb, alog, dfull, nw, e3, bsz, seq):
    t = xbc.shape[0]
    ch = SSM_CHUNK
    per_b = seq // ch
    tile = lambda w: pl.BlockSpec((ch, w), lambda b, s: (b * per_b + s, 0))
    return pl.pallas_call(
        _ssd_kernel,
        out_shape=jax.ShapeDtypeStruct((t, SSM_D_INNER), BF16),
        grid=(bsz, per_b),
        in_specs=[tile(SSM_XBC), tile(SSM_D_INNER), tile(LANES)]
        + [_resident(a.shape) for a in (cw, cb, dtb, alog, dfull, nw, e3)],
        out_specs=tile(SSM_D_INNER),
        scratch_shapes=[pltpu.VMEM((ch + _SSM_HALO, SSM_XBC), F32),
                        pltpu.VMEM((SSM_GROUPS, SSM_STATE, _GROUP_W), F32)],
        compiler_params=_cparams(("parallel", "arbitrary")),
        name="ssd",
    )(xbc, z, dt, cw, cb, dtb, alog, dfull, nw, e3)


_VA_ROWS = 80


def _moba_prep_kernel(q_ref, k_ref, v_ref, qa_ref, ka_ref, va_ref, km_ref, *, seq):
    nb = seq // MOBA_BLOCK
    hd = ATTN_HEAD_DIM
    q = q_ref[...]
    k = k_ref[...].astype(F32)
    for j in range(nb):
        km_ref[j:j + 1, :] = jnp.mean(k[j * MOBA_BLOCK:(j + 1) * MOBA_BLOCK, :], axis=0, keepdims=True)
    km = km_ref[...]
    qt = q.astype(F32).T * (hd ** -0.5)
    vt = v_ref[...].astype(F32).T

    shift = MOBA_BLOCK.bit_length() - 1
    lane_k = lax.broadcasted_iota(jnp.int32, (seq, LANES), 1)
    key_blk = lax.shift_right_logical(lax.broadcasted_iota(jnp.int32, (seq, LANES), 0), shift)
    lane_m = lax.broadcasted_iota(jnp.int32, (nb, LANES), 1)
    sub = lax.broadcasted_iota(jnp.int32, (nb, seq), 0).astype(F32)
    qblk = lax.shift_right_logical(lax.broadcasted_iota(jnp.int32, (1, seq), 1), shift).astype(F32)
    pad_rows = jnp.zeros((LANES - hd - nb, seq), BF16)
    ones_row = jnp.where(lax.broadcasted_iota(jnp.int32, (_VA_ROWS - hd, seq), 0) == 0, 1.0, 0.0).astype(BF16)

    for hh in range(2):
        lo = hh * hd
        in_head = (lane_m >= lo) & (lane_m < lo + hd)
        kmh = jnp.where(in_head, km, 0.0)
        g3 = _dot_nt(jnp.concatenate(_split3(kmh), axis=0), q)
        gate = g3[:nb] + g3[nb:2 * nb] + g3[2 * nb:]
        gate = jnp.where(sub < qblk, gate, -jnp.inf)
        sel = sub == qblk
        for r in range(MOBA_TOPK):
            mx = jnp.max(gate, axis=0, keepdims=True)
            idx = jnp.min(jnp.where(gate == mx, sub, float(nb)), axis=0, keepdims=True)
            hit = sub == idx
            sel = sel | (hit & (qblk > r))
            gate = jnp.where(hit, -jnp.inf, gate)
        bias = jnp.where(sel, 0.0, NEG_BIG).astype(BF16)
        qh = qt[lo:lo + hd, :].astype(BF16)
        if hh == 0:
            qa_ref[0, hh] = jnp.concatenate([qh, bias, pad_rows], axis=0)
            onehot = (lane_k - hd) == key_blk
            ka = jnp.where(lane_k < hd, k, jnp.where(onehot, 1.0, 0.0))
        else:
            qa_ref[0, hh] = jnp.concatenate([bias, pad_rows, qh], axis=0)
            onehot = lane_k == key_blk
            ka = jnp.where(lane_k >= hd, k, jnp.where(onehot, 1.0, 0.0))
        ka_ref[0, 0, :, hh * LANES:(hh + 1) * LANES] = ka.astype(BF16)
        va_ref[0, 0, hh * _VA_ROWS:(hh + 1) * _VA_ROWS, :] = jnp.concatenate(
            [vt[lo:lo + hd, :].astype(BF16), ones_row], axis=0)


def _moba_prep(qkv, bsz, seq):
    pairs = ATTN_HEADS // 2
    nb = seq // MOBA_BLOCK
    assert 2 * nb <= LANES - ATTN_HEAD_DIM
    col = lambda off: pl.BlockSpec((seq, LANES), lambda b, p: (b, off + p))
    return pl.pallas_call(
        functools.partial(_moba_prep_kernel, seq=seq),
        out_shape=[jax.ShapeDtypeStruct((bsz, ATTN_HEADS, LANES, seq), BF16),
                   jax.ShapeDtypeStruct((bsz, pairs, seq, 2 * LANES), BF16),
                   jax.ShapeDtypeStruct((bsz, pairs, 2 * _VA_ROWS, seq), BF16)],
        grid=(bsz, pairs),
        in_specs=[col(0), col(pairs), col(2 * pairs)],
        out_specs=[pl.BlockSpec((1, 2, LANES, seq), lambda b, p: (b, p, 0, 0)),
                   pl.BlockSpec((1, 1, seq, 2 * LANES), lambda b, p: (b, p, 0, 0)),
                   pl.BlockSpec((1, 1, 2 * _VA_ROWS, seq), lambda b, p: (b, p, 0, 0))],
        scratch_shapes=[pltpu.VMEM((nb, LANES), F32)],
        compiler_params=_cparams(("parallel", "parallel")),
        name="moba_prep",
    )(qkv, qkv, qkv)


_MOBA_SWEEP = 4


def _moba_attn_kernel(qa_ref, ka_ref, va_ref, o_ref, m_ref, acc_ref, sa_ref, sb_ref, *, n_chunks):
    i = pl.program_id(2)
    tq = MOBA_BLOCK
    hd = ATTN_HEAD_DIM
    ck = _MOBA_SWEEP * MOBA_BLOCK
    own = lax.shift_right_logical(i, _MOBA_SWEEP.bit_length() - 1)

    zq = jnp.zeros((LANES, tq), BF16)
    qbd = jnp.concatenate([jnp.concatenate([qa_ref[0, 0], zq], axis=1),
                           jnp.concatenate([zq, qa_ref[0, 1]], axis=1)], axis=0)

    def scores(c):
        start = pl.multiple_of(c * ck, ck)
        return _dot(ka_ref[0, 0, pl.ds(start, ck), :], qbd)

    def update(c, st, m, acc):
        start = pl.multiple_of(c * ck, ck)
        vc = va_ref[0, 0, :, pl.ds(start, ck)]
        m_new = jnp.maximum(m, jnp.max(st, axis=0, keepdims=True))
        p = jnp.exp(st - m_new)
        return m_new, acc * jnp.exp(m - m_new) + _dot(vc, p.astype(BF16))

    def own_scores():
        kpos = own * ck + lax.broadcasted_iota(jnp.int32, (ck, 2 * tq), 0)
        qpos = i * tq + (lax.broadcasted_iota(jnp.int32, (ck, 2 * tq), 1) & (tq - 1))
        return jnp.where(kpos <= qpos, scores(own), NEG_BIG)

    def save(m, acc):
        m_ref[...] = m
        acc_ref[...] = acc

    init = (jnp.full((1, 2 * tq), -jnp.inf, F32), jnp.zeros((2 * _VA_ROWS, 2 * tq), F32))

    n = own + 1

    def chunk_at(pos):
        return jnp.where(pos == 0, own, jnp.minimum(pos - 1, n_chunks - 1))

    sa_ref[...] = own_scores()

    def two(k, carry):
        pos = 2 * k
        sb_ref[...] = scores(chunk_at(pos + 1))
        carry = update(chunk_at(pos), sa_ref[...], *carry)
        sa_ref[...] = scores(chunk_at(pos + 2))
        return update(chunk_at(pos + 1), sb_ref[...], *carry)

    save(*lax.fori_loop(0, lax.shift_right_logical(n, 1), two, init))

    @pl.when((n & 1) == 1)
    def _():
        save(*update(chunk_at(n - 1), sa_ref[...], m_ref[...], acc_ref[...]))

    acc = acc_ref[...]
    ot = jnp.concatenate([acc[:hd, :tq] / acc[hd:hd + 1, :tq],
                          acc[_VA_ROWS:_VA_ROWS + hd, tq:] / acc[_VA_ROWS + hd:_VA_ROWS + hd + 1, tq:]], axis=0)
    o_ref[...] = ot.T.astype(BF16)


def _moba_attn(qa, ka, va, bsz, seq):
    pairs = ATTN_HEADS // 2
    nq = seq // MOBA_BLOCK
    ck = _MOBA_SWEEP * MOBA_BLOCK
    assert seq % ck == 0
    return pl.pallas_call(
        functools.partial(_moba_attn_kernel, n_chunks=seq // ck),
        out_shape=jax.ShapeDtypeStruct((bsz * seq, ATTN_WIDTH), BF16),
        grid=(bsz, pairs, nq),
        in_specs=[pl.BlockSpec((1, 2, LANES, MOBA_BLOCK), lambda b, p, i: (b, p, 0, i)),
                  pl.BlockSpec((1, 1, seq, 2 * LANES), lambda b, p, i: (b, p, 0, 0)),
                  pl.BlockSpec((1, 1, 2 * _VA_ROWS, seq), lambda b, p, i: (b, p, 0, 0))],
        out_specs=pl.BlockSpec((MOBA_BLOCK, LANES), lambda b, p, i: (b * nq + i, p)),
        scratch_shapes=[pltpu.VMEM((1, 2 * MOBA_BLOCK), F32), pltpu.VMEM((2 * _VA_ROWS, 2 * MOBA_BLOCK), F32),
                        pltpu.VMEM((ck, 2 * MOBA_BLOCK), F32), pltpu.VMEM((ck, 2 * MOBA_BLOCK), F32)],
        compiler_params=_cparams(("parallel", "parallel", "arbitrary")),
        name="moba_attn",
    )(qa, ka, va)


def _mix_kernel(x_ref, u_ref, y_ref, a_ref, g_ref, mod_ref, wc_ref, ws_ref, wa_ref, wm_ref, o_ref):
    d = D_MODEL
    out_a = _dot(u_ref[...], wc_ref[...])
    out_b = _dot(y_ref[...], ws_ref[...])
    out_c = _dot(a_ref[...], wa_ref[...])
    gt = jax.nn.sigmoid(g_ref[...].astype(F32))
    merged = gt[:, :d] * out_a + gt[:, d:2 * d] * out_b + gt[:, 2 * d:] * out_c
    o_ref[...] = x_ref[...] + mod_ref[0, 2:3, :] * _dot(merged.astype(BF16), wm_ref[...])


def _mix(x2, u, y, a, gate_in, mod_l, wc, ws, wa, wm, seq, tm=512):
    t, d = x2.shape
    per_b = seq // tm
    tile = lambda w: pl.BlockSpec((tm, w), lambda i: (i, 0))
    return pl.pallas_call(
        _mix_kernel,
        out_shape=jax.ShapeDtypeStruct((t, d), F32),
        grid=(t // tm,),
        in_specs=[tile(d), tile(CONF_WIDTH), tile(SSM_D_INNER), tile(ATTN_WIDTH), tile(N_BRANCHES * d),
                  pl.BlockSpec((1, 6, d), lambda i: (i // per_b, 0, 0)),
                  _resident(wc.shape), _resident(ws.shape), _resident(wa.shape), _resident(wm.shape)],
        out_specs=tile(d),
        compiler_params=_cparams(("parallel",)),
        name="mix_out",
    )(x2, u, y, a, gate_in, mod_l, wc, ws, wa, wm)


def _residual_out(x, gate, f, fw, final_norm):
    y = x + gate * f
    if final_norm:
        y = y * lax.rsqrt(jnp.mean(y * y, axis=-1, keepdims=True) + EPS) * fw
    return y


def _ffn_kernel(x_ref, nw_ref, mod_ref, wg_ref, wu_ref, wd_ref, fw_ref, o_ref, hb_ref, acc_ref, *, final_norm):
    f = pl.program_id(1)

    @pl.when(f == 0)
    def _():
        h = _modulated_norm(x_ref[...], nw_ref[...], mod_ref[0, 3:4, :], mod_ref[0, 4:5, :])
        hb_ref[...] = h.astype(BF16)
        acc_ref[...] = jnp.zeros(acc_ref.shape, F32)

    hb = hb_ref[...]
    act = (_silu(_dot(hb, wg_ref[...])) * _dot(hb, wu_ref[...])).astype(BF16)
    acc_ref[...] += _dot(act, wd_ref[...])

    @pl.when(f == pl.num_programs(1) - 1)
    def _():
        o_ref[...] = _residual_out(x_ref[...], mod_ref[0, 5:6, :], acc_ref[...], fw_ref[...], final_norm)


def _ffn(x2, nw, mod_l, wg, wu, wd, fw, seq, *, tm, tf, final_norm):
    t, d = x2.shape
    ff = wg.shape[1]
    per_b = seq // tm
    return pl.pallas_call(
        functools.partial(_ffn_kernel, final_norm=final_norm),
        out_shape=jax.ShapeDtypeStruct((t, d), F32),
        grid=(t // tm, ff // tf),
        in_specs=[pl.BlockSpec((tm, d), lambda i, f: (i, 0)),
                  _resident((1, d)),
                  pl.BlockSpec((1, 6, d), lambda i, f: (i // per_b, 0, 0)),
                  pl.BlockSpec((d, tf), lambda i, f: (0, f)),
                  pl.BlockSpec((d, tf), lambda i, f: (0, f)),
                  pl.BlockSpec((tf, d), lambda i, f: (f, 0)),
                  _resident((1, d))],
        out_specs=pl.BlockSpec((tm, d), lambda i, f: (i, 0)),
        scratch_shapes=[pltpu.VMEM((tm, d), BF16), pltpu.VMEM((tm, d), F32)],
        compiler_params=_cparams(("parallel", "arbitrary")),
        name="ffn_dense",
    )(x2, nw, mod_l, wg, wu, wd, fw)


_MOE_TM = 1024
_PAIRS = D_MODEL // 2
_ROW_WORDS = _PAIRS + LANES


def _pack_pairs(v):
    lo = lax.bitcast_convert_type(v[:, :_PAIRS].astype(BF16).astype(F32), jnp.uint32)
    hi = lax.bitcast_convert_type(v[:, _PAIRS:].astype(BF16).astype(F32), jnp.uint32)
    return hi | lax.shift_right_logical(lo, jnp.uint32(16))


def _unpack_pairs(u):
    lo = lax.bitcast_convert_type(lax.shift_left(u, jnp.uint32(16)), F32)
    hi = lax.bitcast_convert_type(u & jnp.uint32(0xFFFF0000), F32)
    return jnp.concatenate([lo, hi], axis=1)


def _row_copy(src_ref, src_row, dst_ref, dst_row, sem):
    return pltpu.make_async_copy(src_ref.at[pl.ds(src_row, 1), :], dst_ref.at[pl.ds(dst_row, 1), :], sem)


def _moe_route_kernel(x_ref, nw_ref, mod_ref, rwt_ref, rb_ref, xs_ref, dest_ref, cnt_ref,
                      rows_ref, dvm_ref, dsm_ref, carry_ref, row_sem, idx_sem, *, tm, cap, n_experts):
    step = pl.program_id(0)

    @pl.when(step == 0)
    def _():
        carry_ref[...] = jnp.zeros(carry_ref.shape, F32)

    h = _modulated_norm(x_ref[...], nw_ref[...], mod_ref[0, 3:4, :], mod_ref[0, 4:5, :])
    hb = h.astype(BF16)
    l3 = _dot_nt(rwt_ref[...], hb)
    logits = (l3[:LANES] + l3[LANES:2 * LANES] + l3[2 * LANES:] + rb_ref[...])[:n_experts]
    sub = lax.broadcasted_iota(jnp.int32, logits.shape, 0).astype(F32)
    m1 = jnp.max(logits, axis=0, keepdims=True)
    i1 = jnp.min(jnp.where(logits == m1, sub, float(n_experts)), axis=0, keepdims=True)
    rest = jnp.where(sub == i1, -jnp.inf, logits)
    m2 = jnp.max(rest, axis=0, keepdims=True)
    i2 = jnp.min(jnp.where(rest == m2, sub, float(n_experts)), axis=0, keepdims=True)
    e2 = jnp.exp(m2 - m1)
    w1 = 1.0 / (1.0 + e2)
    w2 = e2 / (1.0 + e2)

    hit1 = sub == i1
    hit2 = sub == i2
    picks = jnp.where(hit1 | hit2, 1.0, 0.0)
    picks16 = jnp.concatenate([picks, jnp.zeros((16 - n_experts, tm), F32)], axis=0).astype(BF16)
    before = lax.broadcasted_iota(jnp.int32, (tm, tm), 0) < lax.broadcasted_iota(jnp.int32, (tm, tm), 1)
    prefix = _dot(picks16, jnp.where(before, 1.0, 0.0).astype(BF16))[:n_experts]
    carry = carry_ref[0:n_experts, 0:1]
    rank = prefix + carry
    rank1 = jnp.sum(jnp.where(hit1, rank, 0.0), axis=0, keepdims=True)
    rank2 = jnp.sum(jnp.where(hit2, rank, 0.0), axis=0, keepdims=True)
    total = carry + jnp.sum(picks, axis=1, keepdims=True)
    carry_ref[0:n_experts, :] = jnp.broadcast_to(total, (n_experts, LANES))
    cnt_ref[...] = jnp.broadcast_to(total, (n_experts, LANES))

    dest = jnp.concatenate([i1 * float(cap) + rank1, i2 * float(cap) + rank2,
                            jnp.zeros((6, tm), F32)], axis=0).astype(jnp.int32)
    dest_ref[...] = dest
    dvm_ref[...] = dest
    idx_copy = pltpu.make_async_copy(dvm_ref, dsm_ref, idx_sem)
    idx_copy.start()

    packed = _pack_pairs(h)
    zeros = jnp.zeros((LANES - 2, tm), F32)
    for k, (wa, wb) in enumerate(((w1, w2), (w2, w1))):
        wt = jnp.concatenate([wa, wb, zeros], axis=0).T
        rows_ref[k, :, 0:_PAIRS] = packed
        rows_ref[k, :, _PAIRS:] = lax.bitcast_convert_type(wt, jnp.uint32)
    idx_copy.wait()

    def issue(t, _):
        _row_copy(rows_ref.at[0], t, xs_ref, dsm_ref[0, t], row_sem).start()
        _row_copy(rows_ref.at[1], t, xs_ref, dsm_ref[1, t], row_sem).start()
        return 0

    lax.fori_loop(0, tm, issue, 0, unroll=8)

    def drain(t, _):
        _row_copy(rows_ref.at[0], 0, xs_ref, 0, row_sem).wait()
        _row_copy(rows_ref.at[1], 0, xs_ref, 0, row_sem).wait()
        return 0

    lax.fori_loop(0, tm, drain, 0, unroll=8)


def _moe_route(x2, nw, mod_l, rwt3, rb_col, seq, cap, n_experts, tm=512):
    t, d = x2.shape
    per_b = seq // tm
    return pl.pallas_call(
        functools.partial(_moe_route_kernel, tm=tm, cap=cap, n_experts=n_experts),
        out_shape=[jax.ShapeDtypeStruct((n_experts * cap, _ROW_WORDS), jnp.uint32),
                   jax.ShapeDtypeStruct((8, t), jnp.int32),
                   jax.ShapeDtypeStruct((n_experts, LANES), F32)],
        grid=(t // tm,),
        in_specs=[pl.BlockSpec((tm, d), lambda i: (i, 0)),
                  _resident((1, d)),
                  pl.BlockSpec((1, 6, d), lambda i: (i // per_b, 0, 0)),
                  _resident(rwt3.shape), _resident(rb_col.shape)],
        out_specs=[pl.BlockSpec(memory_space=pl.ANY),
                   pl.BlockSpec((8, tm), lambda i: (0, i)),
                   pl.BlockSpec((n_experts, LANES), lambda i: (0, 0))],
        scratch_shapes=[pltpu.VMEM((2, tm, _ROW_WORDS), jnp.uint32),
                        pltpu.VMEM((8, tm), jnp.int32),
                        pltpu.SMEM((8, tm), jnp.int32),
                        pltpu.VMEM((8, LANES), F32),
                        pltpu.SemaphoreType.DMA, pltpu.SemaphoreType.DMA],
        compiler_params=_cparams(("arbitrary",)),
        name="moe_route",
    )(x2, nw, mod_l, rwt3, rb_col)


def _moe_tails_kernel(cnt_ref, xs_in_ref, xs_ref, zero_ref, sem, *, cap, n_experts):
    del xs_in_ref
    zero_ref[...] = jnp.zeros(zero_ref.shape, jnp.uint32)
    for e in range(n_experts):
        lo = cnt_ref[e]
        hi = (lo + (_MOE_TM - 1)) & (-_MOE_TM)

        def issue(r, _):
            _row_copy(zero_ref, 0, xs_ref, e * cap + r, sem).start()
            return 0

        def drain(r, _):
            _row_copy(zero_ref, 0, xs_ref, 0, sem).wait()
            return 0

        lax.fori_loop(lo, hi, issue, 0)
        lax.fori_loop(lo, hi, drain, 0)


def _moe_tails(counts, xs, cap, n_experts):
    return pl.pallas_call(
        functools.partial(_moe_tails_kernel, cap=cap, n_experts=n_experts),
        out_shape=jax.ShapeDtypeStruct(xs.shape, xs.dtype),
        grid_spec=pltpu.PrefetchScalarGridSpec(
            num_scalar_prefetch=1, grid=(1,),
            in_specs=[pl.BlockSpec(memory_space=pl.ANY)],
            out_specs=pl.BlockSpec(memory_space=pl.ANY),
            scratch_shapes=[pltpu.VMEM((8, _ROW_WORDS), jnp.uint32), pltpu.SemaphoreType.DMA]),
        input_output_aliases={1: 0},
        compiler_params=_cparams(("arbitrary",)),
        name="moe_tails",
    )(counts, xs)


def _moe_expert_kernel(blk_ref, exp_ref, used_ref, x_ref, wg_ref, wu_ref, wd_ref, o_ref, hb_ref, acc_ref):
    del blk_ref, exp_ref
    g = pl.program_id(0)
    f = pl.program_id(1)

    @pl.when(g < used_ref[0])
    def _():
        @pl.when(f == 0)
        def _():
            hb_ref[...] = _unpack_pairs(x_ref[:, 0:_PAIRS]).astype(BF16)
            acc_ref[...] = jnp.zeros(acc_ref.shape, F32)

        hb = hb_ref[...]
        act = (_silu(_dot(hb, wg_ref[0])) * _dot(hb, wu_ref[0])).astype(BF16)
        acc_ref[...] += _dot(act, wd_ref[0])

        @pl.when(f == pl.num_programs(1) - 1)
        def _():
            wcol = lax.bitcast_convert_type(x_ref[:, _PAIRS:_PAIRS + 1], F32)
            o_ref[...] = _pack_pairs(acc_ref[...] * wcol)


def _moe_experts(blk, exp, used, xs, wg, wu, wd, n_tiles, tf=896):
    n_experts, d, ff = wg.shape
    tm = _MOE_TM
    return pl.pallas_call(
        _moe_expert_kernel,
        out_shape=jax.ShapeDtypeStruct((xs.shape[0], _PAIRS), jnp.uint32),
        grid_spec=pltpu.PrefetchScalarGridSpec(
            num_scalar_prefetch=3, grid=(n_tiles, ff // tf),
            in_specs=[pl.BlockSpec((tm, _ROW_WORDS), lambda g, f, blk, exp, used: (blk[g], 0)),
                      pl.BlockSpec((1, d, tf), lambda g, f, blk, exp, used: (exp[g], 0, f)),
                      pl.BlockSpec((1, d, tf), lambda g, f, blk, exp, used: (exp[g], 0, f)),
                      pl.BlockSpec((1, tf, d), lambda g, f, blk, exp, used: (exp[g], f, 0))],
            out_specs=pl.BlockSpec((tm, _PAIRS), lambda g, f, blk, exp, used: (blk[g], 0)),
            scratch_shapes=[pltpu.VMEM((tm, d), BF16), pltpu.VMEM((tm, d), F32)]),
        compiler_params=_cparams(("arbitrary", "arbitrary")),
        name="moe_experts",
    )(blk, exp, used, xs, wg, wu, wd)


def _moe_combine_kernel(dest_ref, x_ref, mod_ref, fw_ref, ys_ref, o_ref, buf_ref, sems, *, tm, n_tok, final_norm):
    step = pl.program_id(0)
    n_steps = pl.num_programs(0)

    def start_gather(s, slot):
        def body(t, _):
            for k in range(2):
                _row_copy(ys_ref, dest_ref[k * n_tok + s * tm + t], buf_ref.at[slot, k], t, sems.at[slot]).start()
            return 0
        lax.fori_loop(0, tm, body, 0, unroll=8)

    def wait_gather(slot):
        def body(t, _):
            for k in range(2):
                _row_copy(ys_ref, 0, buf_ref.at[slot, k], 0, sems.at[slot]).wait()
            return 0
        lax.fori_loop(0, tm, body, 0, unroll=8)

    @pl.when(step == 0)
    def _():
        start_gather(0, 0)

    slot = step & 1

    @pl.when(step + 1 < n_steps)
    def _():
        start_gather(step + 1, 1 - slot)

    wait_gather(slot)
    f = _unpack_pairs(buf_ref[slot, 0]) + _unpack_pairs(buf_ref[slot, 1])
    o_ref[...] = _residual_out(x_ref[...], mod_ref[0, 5:6, :], f, fw_ref[...], final_norm)


def _moe_combine(dest_flat, x2, mod_l, fw, ys, seq, *, final_norm, tm=256):
    t, d = x2.shape
    per_b = seq // tm
    return pl.pallas_call(
        functools.partial(_moe_combine_kernel, tm=tm, n_tok=t, final_norm=final_norm),
        out_shape=jax.ShapeDtypeStruct((t, d), F32),
        grid_spec=pltpu.PrefetchScalarGridSpec(
            num_scalar_prefetch=1, grid=(t // tm,),
            in_specs=[pl.BlockSpec((tm, d), lambda i, dest: (i, 0)),
                      pl.BlockSpec((1, 6, d), lambda i, dest: (i // per_b, 0, 0)),
                      pl.BlockSpec((1, d), lambda i, dest: (0, 0)),
                      pl.BlockSpec(memory_space=pl.ANY)],
            out_specs=pl.BlockSpec((tm, d), lambda i, dest: (i, 0)),
            scratch_shapes=[pltpu.VMEM((2, 2, tm, _PAIRS), jnp.uint32), pltpu.SemaphoreType.DMA((2,))]),
        compiler_params=_cparams(("arbitrary",)),
        name="moe_combine",
    )(dest_flat, x2, mod_l, fw, ys)


def _moe(x2, nw, mod_l, router_w, router_b, wg, wu, wd, fw, seq, *, final_norm):
    t, d = x2.shape
    n_experts = wg.shape[0]
    tm = _MOE_TM
    cap = (t // tm + 1) * tm
    rwt = jnp.zeros((LANES, d), F32).at[:n_experts].set(router_w.T)
    rwt3 = jnp.concatenate(_split3(rwt), axis=0)
    rb_col = jnp.zeros((LANES, 1), F32).at[:n_experts, 0].set(router_b)
    xs, dest, cnt = _moe_route(x2, nw, mod_l, rwt3, rb_col, seq, cap, n_experts)
    counts = cnt[:, 0].astype(jnp.int32)
    xs = _moe_tails(counts, xs, cap, n_experts)

    n_tiles = 2 * t // tm + n_experts
    tiles = (counts + tm - 1) // tm
    first = jnp.cumsum(tiles) - tiles
    used = jnp.sum(tiles)
    g = jnp.minimum(jnp.arange(n_tiles, dtype=jnp.int32), used - 1)
    exp = jnp.sum((g[:, None] >= first[None, :]).astype(jnp.int32), axis=1) - 1
    blk = exp * (cap // tm) + g - first[exp]
    ys = _moe_experts(blk.astype(jnp.int32), exp.astype(jnp.int32), used.reshape(1).astype(jnp.int32),
                      xs, wg, wu, wd, n_tiles)
    dest_flat = jnp.concatenate([dest[0], dest[1]])
    return _moe_combine(dest_flat, x2, mod_l, fw, ys, seq, final_norm=final_norm)


def _pad_lanes(v):
    return jnp.zeros((1, LANES), F32).at[0, :v.shape[0]].set(v)


def kernel(x, c, norm1_w, norm2_w, ada_w, ada_b, w_in, conf_conv_w, conf_conv_b, conf_ln_w, conf_ln_b,
           conf_out, ssm_conv_w, ssm_conv_b, ssm_dt_bias, ssm_A_log, ssm_D, ssm_norm_w, ssm_out, attn_out,
           mix_out, ffn_w_gate, ffn_w_up, ffn_w_down, moe_router_w, moe_router_b, moe_w_gate, moe_w_up,
           moe_w_down, final_norm_w):
    bsz, seq, d = x.shape
    depth = w_in.shape[0]
    t = bsz * seq
    row = lambda v: v.reshape(1, -1).astype(F32)

    mod = _ada_mod(c, ada_w, ada_b).reshape(depth, bsz, 6, d)
    head_of = jnp.arange(SSM_D_INNER) // SSM_HEADDIM
    e1 = (jnp.arange(LANES)[:, None] == head_of[None, :]).astype(BF16)
    e3 = jnp.concatenate([e1, e1, e1], axis=0)

    x2 = x.reshape(t, d)
    edges = [0]
    for wdt in (2 * CONF_WIDTH, SSM_D_INNER, SSM_XBC, SSM_HEADS, 3 * ATTN_WIDTH, N_BRANCHES * D_MODEL):
        edges.append(edges[-1] + wdt)
    for l in range(depth):
        wl = w_in[l]
        parts = [wl[:, edges[i]:edges[i + 1]] for i in range(6)]
        parts[3] = jnp.zeros((d, LANES), F32).at[:, :SSM_HEADS].set(parts[3])
        ws = [p.astype(BF16) for p in parts]
        conf_in, z, xbc, dt, qkv, gate_in = _inproj(x2, row(norm1_w[l]), mod[l], ws, seq)

        u = _conformer(conf_in, conf_conv_w[l], row(conf_conv_b[l]), row(conf_ln_w[l]), row(conf_ln_b[l]),
                       bsz, seq)
        y = _ssd(xbc, z, dt, ssm_conv_w[l], row(ssm_conv_b[l]), _pad_lanes(ssm_dt_bias[l]),
                 _pad_lanes(ssm_A_log[l]), row(jnp.repeat(ssm_D[l], SSM_HEADDIM)), row(ssm_norm_w[l]), e3,
                 bsz, seq)
        qa, ka, va = _moba_prep(qkv, bsz, seq)
        a = _moba_attn(qa, ka, va, bsz, seq)
        x2 = _mix(x2, u, y, a, gate_in, mod[l], conf_out[l].astype(BF16), ssm_out[l].astype(BF16),
                  attn_out[l].astype(BF16), mix_out[l].astype(BF16), seq)

        final = l == depth - 1
        j = l // 2
        if l % 2 == 0:
            x2 = _ffn(x2, row(norm2_w[l]), mod[l], ffn_w_gate[j].astype(BF16), ffn_w_up[j].astype(BF16),
                      ffn_w_down[j].astype(BF16), row(final_norm_w), seq, tm=512, tf=1408, final_norm=final)
        else:
            x2 = _moe(x2, row(norm2_w[l]), mod[l], moe_router_w[j], moe_router_b[j],
                      moe_w_gate[j].astype(BF16), moe_w_up[j].astype(BF16), moe_w_down[j].astype(BF16),
                      row(final_norm_w), seq, final_norm=final)
    return x2.reshape(bsz, seq, d)
```

```python
import functools

import jax
import jax.numpy as jnp
from jax import lax
from jax.experimental import pallas as pl
from jax.experimental.pallas import tpu as pltpu

F32 = jnp.float32
BF16 = jnp.bfloat16

D_MODEL = 1024
CONF_WIDTH = 512
CONF_KERNEL = 31
SSM_D_INNER = 1024
SSM_HEADDIM = 64
SSM_HEADS = 16
SSM_GROUPS = 2
SSM_STATE = 128
SSM_CONV = 4
SSM_CHUNK = 128
SSM_XBC = SSM_D_INNER + 2 * SSM_GROUPS * SSM_STATE
ATTN_HEADS = 8
ATTN_HEAD_DIM = 64
ATTN_WIDTH = ATTN_HEADS * ATTN_HEAD_DIM
MOBA_BLOCK = 256
MOBA_TOPK = 3
N_EXPERTS = 8
N_BRANCHES = 3
EPS = 1e-6

LANES = 128
VMEM_LIMIT = 56 * 1024 * 1024
NEG_BIG = -1e30


def _cparams(sem, vmem=VMEM_LIMIT):
    return pltpu.CompilerParams(dimension_semantics=sem, vmem_limit_bytes=vmem)


def _resident(shape):
    zeros = (0,) * len(shape)
    return pl.BlockSpec(shape, lambda *_: zeros, pipeline_mode=pl.Buffered(1))


def _silu(v):
    return v * jax.nn.sigmoid(v)


def _split2(v):
    hi = v.astype(BF16)
    mid = (v - hi.astype(F32)).astype(BF16)
    return hi, mid


def _split3(v):
    hi = v.astype(BF16)
    r = v - hi.astype(F32)
    mid = r.astype(BF16)
    lo = (r - mid.astype(F32)).astype(BF16)
    return hi, mid, lo


def _dot(a, b):
    return jnp.dot(a, b, preferred_element_type=F32)


def _dot_nt(a, b):
    return lax.dot_general(a, b, (((1,), (1,)), ((), ())), preferred_element_type=F32)


def _dot_tn(a, b):
    return lax.dot_general(a, b, (((0,), (0,)), ((), ())), preferred_element_type=F32)


def _ada_kernel(c_ref, w_ref, b_ref, o_ref):
    c = c_ref[...]
    ch, cm = _split2(_silu(c))
    wh, wm = _split2(w_ref[0])
    o_ref[0] = _dot(ch, wh) + _dot(ch, wm) + _dot(cm, wh) + b_ref[0]


def _ada_mod(c, ada_w, ada_b):
    n_layers, d, n = ada_w.shape
    bsz = c.shape[0]
    rows = 8
    tn = 1536
    cpad = jnp.zeros((rows, d), F32).at[:bsz].set(c)
    out = pl.pallas_call(
        _ada_kernel,
        out_shape=jax.ShapeDtypeStruct((n_layers, rows, n), F32),
        grid=(n_layers, n // tn),
        in_specs=[
            pl.BlockSpec((rows, d), lambda l, j: (0, 0)),
            pl.BlockSpec((1, d, tn), lambda l, j: (l, 0, j)),
            pl.BlockSpec((1, 1, tn), lambda l, j: (l, 0, j)),
        ],
        out_specs=pl.BlockSpec((1, rows, tn), lambda l, j: (l, 0, j)),
        compiler_params=_cparams(("parallel", "parallel")),
        name="ada_mod",
    )(cpad, ada_w, ada_b.reshape(n_layers, 1, n))
    return out[:, :bsz]


def _modulated_norm(x, nw, shift, scale):
    ms = jnp.mean(x * x, axis=-1, keepdims=True)
    return (x * lax.rsqrt(ms + EPS) * nw) * (1.0 + scale) + shift


def _inproj_kernel(x_ref, nw_ref, mod_ref, wc_ref, wz_ref, wx_ref, wd_ref, wq_ref, wg_ref,
                   oc_ref, oz_ref, ox_ref, od_ref, oq_ref, og_ref):
    h = _modulated_norm(x_ref[...], nw_ref[...], mod_ref[0, 0:1, :], mod_ref[0, 1:2, :])
    hb = h.astype(BF16)
    oc_ref[...] = _dot(hb, wc_ref[...]).astype(BF16)
    oz_ref[...] = _dot(hb, wz_ref[...]).astype(BF16)
    ox_ref[...] = _dot(hb, wx_ref[...]).astype(BF16)
    od_ref[...] = _dot(hb, wd_ref[...])
    oq_ref[...] = _dot(hb, wq_ref[...]).astype(BF16)
    og_ref[...] = _dot(hb, wg_ref[...]).astype(BF16)


def _inproj(x2, nw, mod_l, ws, seq, tm=512):
    t, d = x2.shape
    per_b = seq // tm
    widths = [w.shape[1] for w in ws]
    dts = [BF16, BF16, BF16, F32, BF16, BF16]
    return pl.pallas_call(
        _inproj_kernel,
        out_shape=[jax.ShapeDtypeStruct((t, wd), dt) for wd, dt in zip(widths, dts)],
        grid=(t // tm,),
        in_specs=[
            pl.BlockSpec((tm, d), lambda i: (i, 0)),
            _resident((1, d)),
            pl.BlockSpec((1, 6, d), lambda i: (i // per_b, 0, 0)),
        ] + [_resident(w.shape) for w in ws],
        out_specs=[pl.BlockSpec((tm, wd), lambda i: (i, 0)) for wd in widths],
        compiler_params=_cparams(("parallel",)),
        name="inproj",
    )(x2, nw, mod_l, *ws)


_CONF_HALO = 32
_CONF_ROWS = 32
_SUBLANES = 8


def _conf_kernel(cin_ref, w_ref, b_ref, lnw_ref, lnb_ref, o_ref, xg_ref, sh_ref, *, ts):
    s = pl.program_id(1)

    @pl.when(s == 0)
    def _():
        xg_ref[0:_CONF_HALO, :] = jnp.zeros((_CONF_HALO, CONF_WIDTH), F32)

    @pl.when(s > 0)
    def _():
        xg_ref[0:_CONF_HALO, :] = xg_ref[ts:ts + _CONF_HALO, :]

    cin = cin_ref[...].astype(F32)
    xg_ref[_CONF_HALO:_CONF_HALO + ts, :] = cin[:, :CONF_WIDTH] * jax.nn.sigmoid(cin[:, CONF_WIDTH:])
    rows = sh_ref.shape[1]
    for sft in range(1, _SUBLANES):
        sh_ref[sft - 1] = xg_ref[sft:sft + rows, :]
    first = _CONF_HALO - (CONF_KERNEL - 1)
    groups = _CONF_ROWS // _SUBLANES
    bias = jnp.broadcast_to(b_ref[...], (_SUBLANES, CONF_WIDTH))
    for r0 in range(0, ts, _CONF_ROWS):
        accs = [bias] * groups
        for k in range(CONF_KERNEL):
            sft = (first + k) % _SUBLANES
            base = r0 + first + k - sft
            wk = w_ref[k * _SUBLANES:(k + 1) * _SUBLANES, :]
            for g in range(groups):
                lo = base + g * _SUBLANES
                tap = sh_ref[sft - 1, lo:lo + _SUBLANES, :] if sft else xg_ref[lo:lo + _SUBLANES, :]
                accs[g] = accs[g] + wk * tap
        acc = jnp.concatenate(accs, axis=0)
        mu = jnp.mean(acc, axis=-1, keepdims=True)
        xc = acc - mu
        var = jnp.mean(xc * xc, axis=-1, keepdims=True)
        y = xc * lax.rsqrt(var + EPS) * lnw_ref[...] + lnb_ref[...]
        o_ref[r0:r0 + _CONF_ROWS, :] = _silu(y).astype(BF16)


def _conformer(conf_in, cw, cb, lnw, lnb, bsz, seq, ts=512):
    t = conf_in.shape[0]
    per_b = seq // ts
    cw = jnp.repeat(cw, _SUBLANES, axis=0)
    return pl.pallas_call(
        functools.partial(_conf_kernel, ts=ts),
        out_shape=jax.ShapeDtypeStruct((t, CONF_WIDTH), BF16),
        grid=(bsz, per_b),
        in_specs=[
            pl.BlockSpec((ts, 2 * CONF_WIDTH), lambda b, s: (b * per_b + s, 0)),
            _resident(cw.shape), _resident(cb.shape), _resident(lnw.shape), _resident(lnb.shape),
        ],
        out_specs=pl.BlockSpec((ts, CONF_WIDTH), lambda b, s: (b * per_b + s, 0)),
        scratch_shapes=[pltpu.VMEM((ts + _CONF_HALO, CONF_WIDTH), F32),
                        pltpu.VMEM((_SUBLANES - 1, ts + _CONF_HALO - _SUBLANES, CONF_WIDTH), F32)],
        compiler_params=_cparams(("parallel", "arbitrary")),
        name="conformer",
    )(conf_in, cw, cb, lnw, lnb)


_SSM_HALO = 16
_GROUP_W = SSM_D_INNER // SSM_GROUPS


def _ssd_kernel(xbc_ref, z_ref, dt_ref, cw_ref, cb_ref, dtb_ref, alog_ref, dfull_ref, nw_ref,
                e3_ref, shift_ref, o_ref, xh_ref, st_ref):
    ch = SSM_CHUNK
    s = pl.program_id(1)

    @pl.when(s == 0)
    def _():
        xh_ref[0:_SSM_HALO, :] = jnp.zeros((_SSM_HALO, SSM_XBC), BF16)
        st_ref[...] = jnp.zeros(st_ref.shape, F32)

    @pl.when(s > 0)
    def _():
        xh_ref[0:_SSM_HALO, :] = xh_ref[ch:ch + _SSM_HALO, :]

    xcur = xbc_ref[...]
    xh_ref[_SSM_HALO:_SSM_HALO + ch, :] = xcur
    shifted = _dot(shift_ref[...], xh_ref[...])
    acc = cb_ref[...] + cw_ref[SSM_CONV - 1:SSM_CONV, :] * xcur.astype(F32)
    for k in range(SSM_CONV - 1):
        acc = acc + cw_ref[k:k + 1, :] * shifted[k * ch:(k + 1) * ch]
    xc = _silu(acc)
    xs = xc[:, :SSM_D_INNER]
    bm = xc[:, SSM_D_INNER:SSM_D_INNER + SSM_GROUPS * SSM_STATE].astype(BF16)
    cm = xc[:, SSM_D_INNER + SSM_GROUPS * SSM_STATE:].astype(BF16)

    dtr = dt_ref[...] + dtb_ref[...]
    dt = jnp.maximum(dtr, 0.0) + jnp.log(1.0 + jnp.exp(-jnp.abs(dtr)))
    ad = dt * (-jnp.exp(alog_ref[...]))

    row = lax.broadcasted_iota(jnp.int32, (ch, ch), 0)
    col = lax.broadcasted_iota(jnp.int32, (ch, ch), 1)
    causal = row >= col
    tril = jnp.where(causal, 1.0, 0.0).astype(BF16)
    a3 = _dot(tril, jnp.concatenate(_split3(ad), axis=1))
    acum = a3[:, :LANES] + a3[:, LANES:2 * LANES] + a3[:, 2 * LANES:]
    acum_t = acum.T

    dh, dm = _split2(dt)
    lhs = jnp.concatenate([
        jnp.concatenate(_split3(acum), axis=1),
        jnp.concatenate([dh, dm, jnp.zeros_like(dh)], axis=1)], axis=0)
    ex = _dot(lhs, e3_ref[...])
    acum_f = ex[:ch]
    dt_f = ex[ch:]
    xdt = xs * dt_f
    last = acum_f[ch - 1:ch, :]
    xdt_b = xdt.astype(BF16)
    xd_b = (xdt * jnp.exp(last - acum_f)).astype(BF16)
    eacum = jnp.exp(acum_f)
    cdecay = jnp.exp(last)

    lane = lax.broadcasted_iota(jnp.int32, (ch, LANES), 1)
    ys = []
    for g in range(SSM_GROUPS):
        bg = bm[:, g * SSM_STATE:(g + 1) * SSM_STATE]
        cg = cm[:, g * SSM_STATE:(g + 1) * SSM_STATE]
        cb = _dot_nt(cg, bg)
        parts = []
        for pp in range(_GROUP_W // LANES):
            ms = []
            for hh in range(2):
                h = g * (SSM_HEADS // SSM_GROUPS) + 2 * pp + hh
                seg = acum[:, h:h + 1] - acum_t[h:h + 1, :]
                ms.append((cb * jnp.where(causal, jnp.exp(seg), 0.0)).astype(BF16))
            c0 = g * _GROUP_W + pp * LANES
            xp = xdt_b[:, c0:c0 + LANES]
            x2 = jnp.concatenate([jnp.where(lane < SSM_HEADDIM, xp, 0.0).astype(BF16),
                                  jnp.where(lane >= SSM_HEADDIM, xp, 0.0).astype(BF16)], axis=0)
            parts.append(_dot(jnp.concatenate(ms, axis=1), x2))
        gs = slice(g * _GROUP_W, (g + 1) * _GROUP_W)
        st = st_ref[g]
        y_off = _dot(cg, st.astype(BF16)) * eacum[:, gs]
        st_ref[g] = st * cdecay[:, gs] + _dot_tn(bg, xd_b[:, gs])
        ys.append(jnp.concatenate(parts, axis=1) + y_off)
    y = jnp.concatenate(ys, axis=1) + xs * dfull_ref[...]

    gz = y * _silu(z_ref[...].astype(F32))
    outs = []
    for g in range(SSM_GROUPS):
        gg = gz[:, g * _GROUP_W:(g + 1) * _GROUP_W]
        outs.append(gg * lax.rsqrt(jnp.mean(gg * gg, axis=-1, keepdims=True) + EPS))
    o_ref[...] = (jnp.concatenate(outs, axis=1) * nw_ref[...]).astype(BF16)


def _ssd(xbc, z, dt, cw, cb, dtb, alog, dfull, nw, e3, bsz, seq):
    t = xbc.shape[0]
    ch = SSM_CHUNK
    per_b = seq // ch
    tile = lambda w: pl.BlockSpec((ch, w), lambda b, s: (b * per_b + s, 0))
    out_row = jnp.arange((SSM_CONV - 1) * ch)
    src_row = _SSM_HALO + out_row % ch - (SSM_CONV - 1) + out_row // ch
    shift = (jnp.arange(ch + _SSM_HALO)[None, :] == src_row[:, None]).astype(BF16)
    return pl.pallas_call(
        _ssd_kernel,
        out_shape=jax.ShapeDtypeStruct((t, SSM_D_INNER), BF16),
        grid=(bsz, per_b),
        in_specs=[tile(SSM_XBC), tile(SSM_D_INNER), tile(LANES)]
        + [_resident(a.shape) for a in (cw, cb, dtb, alog, dfull, nw, e3, shift)],
        out_specs=tile(SSM_D_INNER),
        scratch_shapes=[pltpu.VMEM((ch + _SSM_HALO, SSM_XBC), BF16),
                        pltpu.VMEM((SSM_GROUPS, SSM_STATE, _GROUP_W), F32)],
        compiler_params=_cparams(("parallel", "arbitrary")),
        name="ssd",
    )(xbc, z, dt, cw, cb, dtb, alog, dfull, nw, e3, shift)


_VA_ROWS = 80


def _moba_prep_kernel(q_ref, k_ref, v_ref, qa_ref, ka_ref, va_ref, km_ref, *, seq):
    nb = seq // MOBA_BLOCK
    hd = ATTN_HEAD_DIM
    q = q_ref[...]
    k = k_ref[...].astype(F32)
    for j in range(nb):
        km_ref[j:j + 1, :] = jnp.mean(k[j * MOBA_BLOCK:(j + 1) * MOBA_BLOCK, :], axis=0, keepdims=True)
    km = km_ref[...]
    qt = q.astype(F32).T * (hd ** -0.5)
    vt = v_ref[...].astype(F32).T

    shift = MOBA_BLOCK.bit_length() - 1
    lane_k = lax.broadcasted_iota(jnp.int32, (seq, LANES), 1)
    key_blk = lax.shift_right_logical(lax.broadcasted_iota(jnp.int32, (seq, LANES), 0), shift)
    lane_m = lax.broadcasted_iota(jnp.int32, (nb, LANES), 1)
    sub = lax.broadcasted_iota(jnp.int32, (nb, seq), 0).astype(F32)
    qblk = lax.shift_right_logical(lax.broadcasted_iota(jnp.int32, (1, seq), 1), shift).astype(F32)
    pad_rows = jnp.zeros((LANES - hd - nb, seq), BF16)
    ones_row = jnp.where(lax.broadcasted_iota(jnp.int32, (_VA_ROWS - hd, seq), 0) == 0, 1.0, 0.0).astype(BF16)

    for hh in range(2):
        lo = hh * hd
        in_head = (lane_m >= lo) & (lane_m < lo + hd)
        kmh = jnp.where(in_head, km, 0.0)
        g3 = _dot_nt(jnp.concatenate(_split3(kmh), axis=0), q)
        gate = g3[:nb] + g3[nb:2 * nb] + g3[2 * nb:]
        gate = jnp.where(sub < qblk, gate, -jnp.inf)
        sel = sub == qblk
        for r in range(MOBA_TOPK):
            mx = jnp.max(gate, axis=0, keepdims=True)
            idx = jnp.min(jnp.where(gate == mx, sub, float(nb)), axis=0, keepdims=True)
            hit = sub == idx
            sel = sel | (hit & (qblk > r))
            gate = jnp.where(hit, -jnp.inf, gate)
        bias = jnp.where(sel, 0.0, NEG_BIG).astype(BF16)
        qh = qt[lo:lo + hd, :].astype(BF16)
        if hh == 0:
            qa_ref[0, hh] = jnp.concatenate([qh, bias, pad_rows], axis=0)
            onehot = (lane_k - hd) == key_blk
            ka = jnp.where(lane_k < hd, k, jnp.where(onehot, 1.0, 0.0))
        else:
            qa_ref[0, hh] = jnp.concatenate([bias, pad_rows, qh], axis=0)
            onehot = lane_k == key_blk
            ka = jnp.where(lane_k >= hd, k, jnp.where(onehot, 1.0, 0.0))
        ka_ref[0, 0, :, hh * LANES:(hh + 1) * LANES] = ka.astype(BF16)
        va_ref[0, 0, hh * _VA_ROWS:(hh + 1) * _VA_ROWS, :] = jnp.concatenate(
            [vt[lo:lo + hd, :].astype(BF16), ones_row], axis=0)


def _moba_prep(qkv, bsz, seq):
    pairs = ATTN_HEADS // 2
    nb = seq // MOBA_BLOCK
    assert 2 * nb <= LANES - ATTN_HEAD_DIM
    col = lambda off: pl.BlockSpec((seq, LANES), lambda b, p: (b, off + p))
    return pl.pallas_call(
        functools.partial(_moba_prep_kernel, seq=seq),
        out_shape=[jax.ShapeDtypeStruct((bsz, ATTN_HEADS, LANES, seq), BF16),
                   jax.ShapeDtypeStruct((bsz, pairs, seq, 2 * LANES), BF16),
                   jax.ShapeDtypeStruct((bsz, pairs, 2 * _VA_ROWS, seq), BF16)],
        grid=(bsz, pairs),
        in_specs=[col(0), col(pairs), col(2 * pairs)],
        out_specs=[pl.BlockSpec((1, 2, LANES, seq), lambda b, p: (b, p, 0, 0)),
                   pl.BlockSpec((1, 1, seq, 2 * LANES), lambda b, p: (b, p, 0, 0)),
                   pl.BlockSpec((1, 1, 2 * _VA_ROWS, seq), lambda b, p: (b, p, 0, 0))],
        scratch_shapes=[pltpu.VMEM((nb, LANES), F32)],
        compiler_params=_cparams(("parallel", "parallel")),
        name="moba_prep",
    )(qkv, qkv, qkv)


_MOBA_SWEEP = 4


def _moba_attn_kernel(qa_ref, ka_ref, va_ref, o_ref, m_ref, acc_ref, sa_ref, sb_ref, *, n_chunks):
    i = pl.program_id(2)
    tq = MOBA_BLOCK
    hd = ATTN_HEAD_DIM
    ck = _MOBA_SWEEP * MOBA_BLOCK
    own = lax.shift_right_logical(i, _MOBA_SWEEP.bit_length() - 1)

    zq = jnp.zeros((LANES, tq), BF16)
    qbd = jnp.concatenate([jnp.concatenate([qa_ref[0, 0], zq], axis=1),
                           jnp.concatenate([zq, qa_ref[0, 1]], axis=1)], axis=0)

    def scores(c):
        start = pl.multiple_of(c * ck, ck)
        return _dot(ka_ref[0, 0, pl.ds(start, ck), :], qbd)

    def update(c, st, m, acc):
        start = pl.multiple_of(c * ck, ck)
        vc = va_ref[0, 0, :, pl.ds(start, ck)]
        m_new = jnp.maximum(m, jnp.max(st, axis=0, keepdims=True))
        p = jnp.exp(st - m_new).astype(BF16)
        alpha = jnp.exp(m - m_new)
        halves = []
        for hh in range(2):
            rows = slice(hh * _VA_ROWS, (hh + 1) * _VA_ROWS)
            cols = slice(hh * tq, (hh + 1) * tq)
            halves.append(acc[rows] * alpha[:, cols] + _dot(vc[rows], p[:, cols]))
        return m_new, jnp.concatenate(halves, axis=0)

    def own_scores():
        kpos = own * ck + lax.broadcasted_iota(jnp.int32, (ck, 2 * tq), 0)
        qpos = i * tq + (lax.broadcasted_iota(jnp.int32, (ck, 2 * tq), 1) & (tq - 1))
        return jnp.where(kpos <= qpos, scores(own), NEG_BIG)

    def save(m, acc):
        m_ref[...] = m
        acc_ref[...] = acc

    init = (jnp.full((1, 2 * tq), -jnp.inf, F32), jnp.zeros((2 * _VA_ROWS, tq), F32))

    n = own + 1

    def chunk_at(pos):
        return jnp.where(pos == 0, own, jnp.minimum(pos - 1, n_chunks - 1))

    sa_ref[...] = own_scores()

    def two(k, carry):
        pos = 2 * k
        sb_ref[...] = scores(chunk_at(pos + 1))
        carry = update(chunk_at(pos), sa_ref[...], *carry)
        sa_ref[...] = scores(chunk_at(pos + 2))
        return update(chunk_at(pos + 1), sb_ref[...], *carry)

    save(*lax.fori_loop(0, lax.shift_right_logical(n, 1), two, init))

    @pl.when((n & 1) == 1)
    def _():
        save(*update(chunk_at(n - 1), sa_ref[...], m_ref[...], acc_ref[...]))

    acc = acc_ref[...]
    ot = jnp.concatenate([acc[:hd] / acc[hd:hd + 1],
                          acc[_VA_ROWS:_VA_ROWS + hd] / acc[_VA_ROWS + hd:_VA_ROWS + hd + 1]], axis=0)
    o_ref[...] = ot.T.astype(BF16)


def _moba_attn(qa, ka, va, bsz, seq):
    pairs = ATTN_HEADS // 2
    nq = seq // MOBA_BLOCK
    ck = _MOBA_SWEEP * MOBA_BLOCK
    assert seq % ck == 0
    return pl.pallas_call(
        functools.partial(_moba_attn_kernel, n_chunks=seq // ck),
        out_shape=jax.ShapeDtypeStruct((bsz * seq, ATTN_WIDTH), BF16),
        grid=(bsz, pairs, nq),
        in_specs=[pl.BlockSpec((1, 2, LANES, MOBA_BLOCK), lambda b, p, i: (b, p, 0, i)),
                  pl.BlockSpec((1, 1, seq, 2 * LANES), lambda b, p, i: (b, p, 0, 0)),
                  pl.BlockSpec((1, 1, 2 * _VA_ROWS, seq), lambda b, p, i: (b, p, 0, 0))],
        out_specs=pl.BlockSpec((MOBA_BLOCK, LANES), lambda b, p, i: (b * nq + i, p)),
        scratch_shapes=[pltpu.VMEM((1, 2 * MOBA_BLOCK), F32), pltpu.VMEM((2 * _VA_ROWS, MOBA_BLOCK), F32),
                        pltpu.VMEM((ck, 2 * MOBA_BLOCK), F32), pltpu.VMEM((ck, 2 * MOBA_BLOCK), F32)],
        compiler_params=_cparams(("parallel", "parallel", "arbitrary")),
        name="moba_attn",
    )(qa, ka, va)


def _mix_kernel(x_ref, u_ref, y_ref, a_ref, g_ref, mod_ref, wc_ref, ws_ref, wa_ref, wm_ref, o_ref):
    d = D_MODEL
    out_a = _dot(u_ref[...], wc_ref[...])
    out_b = _dot(y_ref[...], ws_ref[...])
    out_c = _dot(a_ref[...], wa_ref[...])
    gt = jax.nn.sigmoid(g_ref[...].astype(F32))
    merged = gt[:, :d] * out_a + gt[:, d:2 * d] * out_b + gt[:, 2 * d:] * out_c
    o_ref[...] = x_ref[...] + mod_ref[0, 2:3, :] * _dot(merged.astype(BF16), wm_ref[...])


def _mix(x2, u, y, a, gate_in, mod_l, wc, ws, wa, wm, seq, tm=512):
    t, d = x2.shape
    per_b = seq // tm
    tile = lambda w: pl.BlockSpec((tm, w), lambda i: (i, 0))
    return pl.pallas_call(
        _mix_kernel,
        out_shape=jax.ShapeDtypeStruct((t, d), F32),
        grid=(t // tm,),
        in_specs=[tile(d), tile(CONF_WIDTH), tile(SSM_D_INNER), tile(ATTN_WIDTH), tile(N_BRANCHES * d),
                  pl.BlockSpec((1, 6, d), lambda i: (i // per_b, 0, 0)),
                  _resident(wc.shape), _resident(ws.shape), _resident(wa.shape), _resident(wm.shape)],
        out_specs=tile(d),
        compiler_params=_cparams(("parallel",)),
        name="mix_out",
    )(x2, u, y, a, gate_in, mod_l, wc, ws, wa, wm)


def _residual_out(x, gate, f, fw, final_norm):
    y = x + gate * f
    if final_norm:
        y = y * lax.rsqrt(jnp.mean(y * y, axis=-1, keepdims=True) + EPS) * fw
    return y


def _ffn_kernel(x_ref, nw_ref, mod_ref, wg_ref, wu_ref, wd_ref, fw_ref, o_ref, hb_ref, acc_ref, *, final_norm):
    f = pl.program_id(1)

    @pl.when(f == 0)
    def _():
        h = _modulated_norm(x_ref[...], nw_ref[...], mod_ref[0, 3:4, :], mod_ref[0, 4:5, :])
        hb_ref[...] = h.astype(BF16)
        acc_ref[...] = jnp.zeros(acc_ref.shape, F32)

    hb = hb_ref[...]
    act = (_silu(_dot(hb, wg_ref[...])) * _dot(hb, wu_ref[...])).astype(BF16)
    acc_ref[...] += _dot(act, wd_ref[...])

    @pl.when(f == pl.num_programs(1) - 1)
    def _():
        o_ref[...] = _residual_out(x_ref[...], mod_ref[0, 5:6, :], acc_ref[...], fw_ref[...], final_norm)


def _ffn(x2, nw, mod_l, wg, wu, wd, fw, seq, *, tm, tf, final_norm):
    t, d = x2.shape
    ff = wg.shape[1]
    per_b = seq // tm
    if tf == ff:
        w_specs = [_resident((d, ff)), _resident((d, ff)), _resident((ff, d))]
    else:
        w_specs = [pl.BlockSpec((d, tf), lambda i, f: (0, f)), pl.BlockSpec((d, tf), lambda i, f: (0, f)),
                   pl.BlockSpec((tf, d), lambda i, f: (f, 0))]
    return pl.pallas_call(
        functools.partial(_ffn_kernel, final_norm=final_norm),
        out_shape=jax.ShapeDtypeStruct((t, d), F32),
        grid=(t // tm, ff // tf),
        in_specs=[pl.BlockSpec((tm, d), lambda i, f: (i, 0)),
                  _resident((1, d)),
                  pl.BlockSpec((1, 6, d), lambda i, f: (i // per_b, 0, 0))] + w_specs + [
                  _resident((1, d))],
        out_specs=pl.BlockSpec((tm, d), lambda i, f: (i, 0)),
        scratch_shapes=[pltpu.VMEM((tm, d), BF16), pltpu.VMEM((tm, d), F32)],
        compiler_params=_cparams(("parallel", "arbitrary")),
        name="ffn_dense",
    )(x2, nw, mod_l, wg, wu, wd, fw)


_MOE_TM = 512
_PAIRS = D_MODEL // 2
_ROW_WORDS = _PAIRS + LANES


def _pack_pairs(v):
    lo = lax.bitcast_convert_type(v[:, :_PAIRS].astype(BF16).astype(F32), jnp.uint32)
    hi = lax.bitcast_convert_type(v[:, _PAIRS:].astype(BF16).astype(F32), jnp.uint32)
    return hi | lax.shift_right_logical(lo, jnp.uint32(16))


def _unpack_pairs(u):
    lo = lax.bitcast_convert_type(lax.shift_left(u, jnp.uint32(16)), F32)
    hi = lax.bitcast_convert_type(u & jnp.uint32(0xFFFF0000), F32)
    return jnp.concatenate([lo, hi], axis=1)


def _row_copy(src_ref, src_row, dst_ref, dst_row, sem):
    return pltpu.make_async_copy(src_ref.at[pl.ds(src_row, 1), :], dst_ref.at[pl.ds(dst_row, 1), :], sem)


def _moe_route_kernel(x_ref, nw_ref, mod_ref, rwt_ref, rb_ref, xs_ref, dest_ref, cnt_ref,
                      rows_ref, dvm_ref, dsm_ref, carry_ref, row_sem, idx_sem, *, tm, cap, n_experts):
    step = pl.program_id(0)

    @pl.when(step == 0)
    def _():
        carry_ref[...] = jnp.zeros(carry_ref.shape, F32)

    h = _modulated_norm(x_ref[...], nw_ref[...], mod_ref[0, 3:4, :], mod_ref[0, 4:5, :])
    hb = h.astype(BF16)
    l3 = _dot_nt(rwt_ref[...], hb)
    logits = (l3[:LANES] + l3[LANES:2 * LANES] + l3[2 * LANES:] + rb_ref[...])[:n_experts]
    sub = lax.broadcasted_iota(jnp.int32, logits.shape, 0).astype(F32)
    m1 = jnp.max(logits, axis=0, keepdims=True)
    i1 = jnp.min(jnp.where(logits == m1, sub, float(n_experts)), axis=0, keepdims=True)
    rest = jnp.where(sub == i1, -jnp.inf, logits)
    m2 = jnp.max(rest, axis=0, keepdims=True)
    i2 = jnp.min(jnp.where(rest == m2, sub, float(n_experts)), axis=0, keepdims=True)
    e2 = jnp.exp(m2 - m1)
    w1 = 1.0 / (1.0 + e2)
    w2 = e2 / (1.0 + e2)

    hit1 = sub == i1
    hit2 = sub == i2
    picks = jnp.where(hit1 | hit2, 1.0, 0.0)
    picks16 = jnp.concatenate([picks, jnp.zeros((16 - n_experts, tm), F32)], axis=0).astype(BF16)
    before = lax.broadcasted_iota(jnp.int32, (tm, tm), 0) < lax.broadcasted_iota(jnp.int32, (tm, tm), 1)
    prefix = _dot(picks16, jnp.where(before, 1.0, 0.0).astype(BF16))[:n_experts]
    carry = carry_ref[0:n_experts, 0:1]
    rank = prefix + carry
    rank1 = jnp.sum(jnp.where(hit1, rank, 0.0), axis=0, keepdims=True)
    rank2 = jnp.sum(jnp.where(hit2, rank, 0.0), axis=0, keepdims=True)
    total = carry + jnp.sum(picks, axis=1, keepdims=True)
    carry_ref[0:n_experts, :] = jnp.broadcast_to(total, (n_experts, LANES))
    cnt_ref[...] = jnp.broadcast_to(total, (n_experts, LANES))

    dest = jnp.concatenate([i1 * float(cap) + rank1, i2 * float(cap) + rank2,
                            jnp.zeros((6, tm), F32)], axis=0).astype(jnp.int32)
    dest_ref[...] = dest
    dvm_ref[...] = dest
    idx_copy = pltpu.make_async_copy(dvm_ref, dsm_ref, idx_sem)
    idx_copy.start()

    packed = _pack_pairs(h)
    zeros = jnp.zeros((LANES - 2, tm), F32)
    for k, (wa, wb) in enumerate(((w1, w2), (w2, w1))):
        wt = jnp.concatenate([wa, wb, zeros], axis=0).T
        rows_ref[k, :, 0:_PAIRS] = packed
        rows_ref[k, :, _PAIRS:] = lax.bitcast_convert_type(wt, jnp.uint32)
    idx_copy.wait()

    def issue(t, _):
        _row_copy(rows_ref.at[0], t, xs_ref, dsm_ref[0, t], row_sem).start()
        _row_copy(rows_ref.at[1], t, xs_ref, dsm_ref[1, t], row_sem).start()
        return 0

    lax.fori_loop(0, tm, issue, 0, unroll=8)

    def drain(t, _):
        _row_copy(rows_ref.at[0], 0, xs_ref, 0, row_sem).wait()
        _row_copy(rows_ref.at[1], 0, xs_ref, 0, row_sem).wait()
        return 0

    lax.fori_loop(0, tm, drain, 0, unroll=8)


def _moe_route(x2, nw, mod_l, rwt3, rb_col, seq, cap, n_experts, tm=512):
    t, d = x2.shape
    per_b = seq // tm
    return pl.pallas_call(
        functools.partial(_moe_route_kernel, tm=tm, cap=cap, n_experts=n_experts),
        out_shape=[jax.ShapeDtypeStruct((n_experts * cap, _ROW_WORDS), jnp.uint32),
                   jax.ShapeDtypeStruct((8, t), jnp.int32),
                   jax.ShapeDtypeStruct((n_experts, LANES), F32)],
        grid=(t // tm,),
        in_specs=[pl.BlockSpec((tm, d), lambda i: (i, 0)),
                  _resident((1, d)),
                  pl.BlockSpec((1, 6, d), lambda i: (i // per_b, 0, 0)),
                  _resident(rwt3.shape), _resident(rb_col.shape)],
        out_specs=[pl.BlockSpec(memory_space=pl.ANY),
                   pl.BlockSpec((8, tm), lambda i: (0, i)),
                   pl.BlockSpec((n_experts, LANES), lambda i: (0, 0))],
        scratch_shapes=[pltpu.VMEM((2, tm, _ROW_WORDS), jnp.uint32),
                        pltpu.VMEM((8, tm), jnp.int32),
                        pltpu.SMEM((8, tm), jnp.int32),
                        pltpu.VMEM((8, LANES), F32),
                        pltpu.SemaphoreType.DMA, pltpu.SemaphoreType.DMA],
        compiler_params=_cparams(("arbitrary",)),
        name="moe_route",
    )(x2, nw, mod_l, rwt3, rb_col)


def _moe_tails_kernel(cnt_ref, xs_in_ref, xs_ref, zero_ref, sem, *, cap, n_experts):
    del xs_in_ref
    zero_ref[...] = jnp.zeros(zero_ref.shape, jnp.uint32)
    for e in range(n_experts):
        lo = cnt_ref[e]
        hi = (lo + (_MOE_TM - 1)) & (-_MOE_TM)

        def issue(r, _):
            _row_copy(zero_ref, 0, xs_ref, e * cap + r, sem).start()
            return 0

        def drain(r, _):
            _row_copy(zero_ref, 0, xs_ref, 0, sem).wait()
            return 0

        lax.fori_loop(lo, hi, issue, 0)
        lax.fori_loop(lo, hi, drain, 0)


def _moe_tails(counts, xs, cap, n_experts):
    return pl.pallas_call(
        functools.partial(_moe_tails_kernel, cap=cap, n_experts=n_experts),
        out_shape=jax.ShapeDtypeStruct(xs.shape, xs.dtype),
        grid_spec=pltpu.PrefetchScalarGridSpec(
            num_scalar_prefetch=1, grid=(1,),
            in_specs=[pl.BlockSpec(memory_space=pl.ANY)],
            out_specs=pl.BlockSpec(memory_space=pl.ANY),
            scratch_shapes=[pltpu.VMEM((8, _ROW_WORDS), jnp.uint32), pltpu.SemaphoreType.DMA]),
        input_output_aliases={1: 0},
        compiler_params=_cparams(("arbitrary",)),
        name="moe_tails",
    )(counts, xs)


def _moe_expert_kernel(blk_ref, exp_ref, used_ref, x_ref, wg_ref, wu_ref, wd_ref, o_ref, hb_ref, acc_ref):
    del blk_ref, exp_ref
    g = pl.program_id(0)
    f = pl.program_id(1)

    @pl.when(g < used_ref[0])
    def _():
        @pl.when(f == 0)
        def _():
            hb_ref[...] = _unpack_pairs(x_ref[:, 0:_PAIRS]).astype(BF16)
            acc_ref[...] = jnp.zeros(acc_ref.shape, F32)

        hb = hb_ref[...]
        act = (_silu(_dot(hb, wg_ref[0])) * _dot(hb, wu_ref[0])).astype(BF16)
        acc_ref[...] += _dot(act, wd_ref[0])

        @pl.when(f == pl.num_programs(1) - 1)
        def _():
            wcol = lax.bitcast_convert_type(x_ref[:, _PAIRS:_PAIRS + 1], F32)
            o_ref[...] = _pack_pairs(acc_ref[...] * wcol)


def _moe_experts(blk, exp, used, xs, wg, wu, wd, n_tiles, tf=1792):
    n_experts, d, ff = wg.shape
    tm = _MOE_TM
    return pl.pallas_call(
        _moe_expert_kernel,
        out_shape=jax.ShapeDtypeStruct((xs.shape[0], _PAIRS), jnp.uint32),
        grid_spec=pltpu.PrefetchScalarGridSpec(
            num_scalar_prefetch=3, grid=(n_tiles, ff // tf),
            in_specs=[pl.BlockSpec((tm, _ROW_WORDS), lambda g, f, blk, exp, used: (blk[g], 0)),
                      pl.BlockSpec((1, d, tf), lambda g, f, blk, exp, used: (exp[g], 0, f)),
                      pl.BlockSpec((1, d, tf), lambda g, f, blk, exp, used: (exp[g], 0, f)),
                      pl.BlockSpec((1, tf, d), lambda g, f, blk, exp, used: (exp[g], f, 0))],
            out_specs=pl.BlockSpec((tm, _PAIRS), lambda g, f, blk, exp, used: (blk[g], 0)),
            scratch_shapes=[pltpu.VMEM((tm, d), BF16), pltpu.VMEM((tm, d), F32)]),
        compiler_params=_cparams(("arbitrary", "arbitrary")),
        name="moe_experts",
    )(blk, exp, used, xs, wg, wu, wd)


def _moe_combine_kernel(dest_ref, x_ref, mod_ref, fw_ref, ys_ref, o_ref, buf_ref, sems, *, tm, n_tok, final_norm):
    step = pl.program_id(0)
    n_steps = pl.num_programs(0)

    def start_gather(s, slot):
        def body(t, _):
            for k in range(2):
                _row_copy(ys_ref, dest_ref[k * n_tok + s * tm + t], buf_ref.at[slot, k], t, sems.at[slot]).start()
            return 0
        lax.fori_loop(0, tm, body, 0, unroll=8)

    def wait_gather(slot):
        def body(t, _):
            for k in range(2):
                _row_copy(ys_ref, 0, buf_ref.at[slot, k], 0, sems.at[slot]).wait()
            return 0
        lax.fori_loop(0, tm, body, 0, unroll=8)

    @pl.when(step == 0)
    def _():
        start_gather(0, 0)

    slot = step & 1

    @pl.when(step + 1 < n_steps)
    def _():
        start_gather(step + 1, 1 - slot)

    wait_gather(slot)
    f = _unpack_pairs(buf_ref[slot, 0]) + _unpack_pairs(buf_ref[slot, 1])
    o_ref[...] = _residual_out(x_ref[...], mod_ref[0, 5:6, :], f, fw_ref[...], final_norm)


def _moe_combine(dest_flat, x2, mod_l, fw, ys, seq, *, final_norm, tm=256):
    t, d = x2.shape
    per_b = seq // tm
    return pl.pallas_call(
        functools.partial(_moe_combine_kernel, tm=tm, n_tok=t, final_norm=final_norm),
        out_shape=jax.ShapeDtypeStruct((t, d), F32),
        grid_spec=pltpu.PrefetchScalarGridSpec(
            num_scalar_prefetch=1, grid=(t // tm,),
            in_specs=[pl.BlockSpec((tm, d), lambda i, dest: (i, 0)),
                      pl.BlockSpec((1, 6, d), lambda i, dest: (i // per_b, 0, 0)),
                      pl.BlockSpec((1, d), lambda i, dest: (0, 0)),
                      pl.BlockSpec(memory_space=pl.ANY)],
            out_specs=pl.BlockSpec((tm, d), lambda i, dest: (i, 0)),
            scratch_shapes=[pltpu.VMEM((2, 2, tm, _PAIRS), jnp.uint32), pltpu.SemaphoreType.DMA((2,))]),
        compiler_params=_cparams(("arbitrary",)),
        name="moe_combine",
    )(dest_flat, x2, mod_l, fw, ys)


def _moe(x2, nw, mod_l, router_w, router_b, wg, wu, wd, fw, seq, *, final_norm):
    t, d = x2.shape
    n_experts = wg.shape[0]
    tm = _MOE_TM
    cap = (t // tm + 1) * tm
    rwt = jnp.zeros((LANES, d), F32).at[:n_experts].set(router_w.T)
    rwt3 = jnp.concatenate(_split3(rwt), axis=0)
    rb_col = jnp.zeros((LANES, 1), F32).at[:n_experts, 0].set(router_b)
    xs, dest, cnt = _moe_route(x2, nw, mod_l, rwt3, rb_col, seq, cap, n_experts)
    counts = cnt[:, 0].astype(jnp.int32)
    xs = _moe_tails(counts, xs, cap, n_experts)

    n_tiles = 2 * t // tm + n_experts
    tiles = (counts + tm - 1) // tm
    first = jnp.cumsum(tiles) - tiles
    used = jnp.sum(tiles)
    g = jnp.minimum(jnp.arange(n_tiles, dtype=jnp.int32), used - 1)
    exp = jnp.sum((g[:, None] >= first[None, :]).astype(jnp.int32), axis=1) - 1
    blk = exp * (cap // tm) + g - first[exp]
    ys = _moe_experts(blk.astype(jnp.int32), exp.astype(jnp.int32), used.reshape(1).astype(jnp.int32),
                      xs, wg, wu, wd, n_tiles)
    dest_flat = jnp.concatenate([dest[0], dest[1]])
    return _moe_combine(dest_flat, x2, mod_l, fw, ys, seq, final_norm=final_norm)


def _pad_lanes(v):
    return jnp.zeros((1, LANES), F32).at[0, :v.shape[0]].set(v)


def kernel(x, c, norm1_w, norm2_w, ada_w, ada_b, w_in, conf_conv_w, conf_conv_b, conf_ln_w, conf_ln_b,
           conf_out, ssm_conv_w, ssm_conv_b, ssm_dt_bias, ssm_A_log, ssm_D, ssm_norm_w, ssm_out, attn_out,
           mix_out, ffn_w_gate, ffn_w_up, ffn_w_down, moe_router_w, moe_router_b, moe_w_gate, moe_w_up,
           moe_w_down, final_norm_w):
    bsz, seq, d = x.shape
    depth = w_in.shape[0]
    t = bsz * seq
    row = lambda v: v.reshape(1, -1).astype(F32)

    mod = _ada_mod(c, ada_w, ada_b).reshape(depth, bsz, 6, d)
    head_of = jnp.arange(SSM_D_INNER) // SSM_HEADDIM
    e1 = (jnp.arange(LANES)[:, None] == head_of[None, :]).astype(BF16)
    e3 = jnp.concatenate([e1, e1, e1], axis=0)

    x2 = x.reshape(t, d)
    edges = [0]
    for wdt in (2 * CONF_WIDTH, SSM_D_INNER, SSM_XBC, SSM_HEADS, 3 * ATTN_WIDTH, N_BRANCHES * D_MODEL):
        edges.append(edges[-1] + wdt)
    for l in range(depth):
        wl = w_in[l]
        parts = [wl[:, edges[i]:edges[i + 1]] for i in range(6)]
        parts[3] = jnp.zeros((d, LANES), F32).at[:, :SSM_HEADS].set(parts[3])
        ws = [p.astype(BF16) for p in parts]
        conf_in, z, xbc, dt, qkv, gate_in = _inproj(x2, row(norm1_w[l]), mod[l], ws, seq)

        u = _conformer(conf_in, conf_conv_w[l], row(conf_conv_b[l]), row(conf_ln_w[l]), row(conf_ln_b[l]),
                       bsz, seq)
        y = _ssd(xbc, z, dt, ssm_conv_w[l], row(ssm_conv_b[l]), _pad_lanes(ssm_dt_bias[l]),
                 _pad_lanes(ssm_A_log[l]), row(jnp.repeat(ssm_D[l], SSM_HEADDIM)), row(ssm_norm_w[l]), e3,
                 bsz, seq)
        qa, ka, va = _moba_prep(qkv, bsz, seq)
        a = _moba_attn(qa, ka, va, bsz, seq)
        x2 = _mix(x2, u, y, a, gate_in, mod[l], conf_out[l].astype(BF16), ssm_out[l].astype(BF16),
                  attn_out[l].astype(BF16), mix_out[l].astype(BF16), seq)

        final = l == depth - 1
        j = l // 2
        if l % 2 == 0:
            x2 = _ffn(x2, row(norm2_w[l]), mod[l], ffn_w_gate[j].astype(BF16), ffn_w_up[j].astype(BF16),
                      ffn_w_down[j].astype(BF16), row(final_norm_w), seq, tm=512, tf=2816, final_norm=final)
        else:
            x2 = _moe(x2, row(norm2_w[l]), mod[l], moe_router_w[j], moe_router_b[j],
                      moe_w_gate[j].astype(BF16), moe_w_up[j].astype(BF16), moe_w_down[j].astype(BF16),
                      row(final_norm_w), seq, final_norm=final)
    return x2.reshape(bsz, seq, d)
```
